```python
import math
import jax, jax.numpy as jnp
from jax import lax
import numpy as np

D_MODEL = 1024
BATCH = 8
SEQ = 2048
DEPTH = 1

PLE_DIM = 256
D_FF = 2816
EPS = 1e-6
GLA_HEADS = 4
GLA_DK = 128
GLA_DV = 256
GLA_LOWRANK = 16
GLA_TAU = 16.0
GLA_CHUNK = 64
DSA_HEADS = 8
DSA_LATENT = 128
IDX_HEADS = 8
IDX_DIM = 64
TOPK_MAX = 256
Q_BLOCK = 128
REL_BUCKETS = 32
REL_MAX_DIST = 128

IN_SIZES = (
    GLA_HEADS * GLA_DK,
    GLA_HEADS * GLA_DK,
    GLA_HEADS * GLA_DV,
    GLA_HEADS * GLA_DV,
    GLA_LOWRANK,
    DSA_HEADS * DSA_LATENT,
    DSA_LATENT,
    IDX_HEADS * IDX_DIM,
    IDX_DIM,
    IDX_HEADS,
    2 * D_MODEL,
)
D_IN = sum(IN_SIZES)

kernel_name = "gla_dsa_parallel_hybrid_macaron"


def rms_norm(x, g):
    xf = x.astype(jnp.float32)
    y = xf * lax.rsqrt(jnp.mean(xf * xf, axis=-1, keepdims=True) + EPS)
    return (y * g.astype(jnp.float32)).astype(x.dtype)


def swiglu(x, w_in, w_out):
    gate, up = jnp.split(x @ w_in, 2, axis=-1)
    return (jax.nn.silu(gate) * up) @ w_out


def t5_bucket(dist):
    max_exact = REL_BUCKETS // 2
    d = jnp.maximum(dist, 0)
    ratio = jnp.maximum(d, 1).astype(jnp.float32) / max_exact
    large = max_exact + (jnp.log(ratio) / math.log(REL_MAX_DIST / max_exact)
                         * (REL_BUCKETS - max_exact)).astype(jnp.int32)
    large = jnp.minimum(large, REL_BUCKETS - 1)
    return jnp.where(d < max_exact, d, large)


def gla_chunked(q, k, v, log_a):
    B, H, S, dk = q.shape
    dv = v.shape[-1]
    C = GLA_CHUNK
    N = S // C

    def to_chunks(t):
        return jnp.moveaxis(t.reshape(B, H, N, C, t.shape[-1]), 2, 0)

    causal = jnp.tril(jnp.ones((C, C), dtype=bool))[:, :, None]

    def step(state, inp):
        qi, ki, vi, gi = inp
        b = jnp.cumsum(gi.astype(jnp.float32), axis=2)
        o_inter = jnp.einsum('bhtd,bhde->bhte', qi * jnp.exp(b), state)
        diff = b[:, :, :, None, :] - b[:, :, None, :, :]
        decay = jnp.exp(jnp.where(causal, diff, -jnp.inf))
        scores = jnp.einsum('bhtd,bhsd,bhtsd->bhts', qi, ki, decay)
        o_intra = jnp.einsum('bhts,bhse->bhte', scores, vi)
        b_last = b[:, :, -1:, :]
        k_dec = ki * jnp.exp(b_last - b)
        new_state = (jnp.exp(b_last[:, :, 0, :])[..., None] * state
                     + jnp.einsum('bhsd,bhse->bhde', k_dec, vi))
        return new_state, o_inter + o_intra

    state0 = jnp.zeros((B, H, dk, dv), jnp.float32)
    _, out = lax.scan(step, state0, (to_chunks(q), to_chunks(k), to_chunks(v), to_chunks(log_a)))
    return jnp.moveaxis(out, 0, 2).reshape(B, H, S, dv)


def dsa_attention(q, c_kv, q_idx, k_idx, w_idx, rel_bias):
    B, S, H, dc = q.shape
    top_k = min(TOPK_MAX, S // 4)
    nb = S // Q_BLOCK
    key_pos = jnp.arange(S, dtype=jnp.int32)
    bidx = jnp.arange(B)[:, None, None]

    def blocks(t):
        return jnp.moveaxis(t.reshape(B, nb, Q_BLOCK, *t.shape[2:]), 1, 0)

    def one_block(args):
        blk, qb, qib, wb = args
        q_pos = blk * Q_BLOCK + jnp.arange(Q_BLOCK, dtype=jnp.int32)
        dots = jnp.einsum('bthd,bsd->bths', qib, k_idx).astype(jnp.float32) * (IDX_DIM ** -0.5)
        w = wb.astype(jnp.float32) * (IDX_HEADS ** -0.5)
        score = jnp.einsum('bths,bth->bts', jax.nn.relu(dots), w)
        causal = key_pos[None, :] <= q_pos[:, None]
        score = jnp.where(causal[None], score, -jnp.inf)
        _, sel = lax.top_k(score, top_k)
        valid = sel <= q_pos[None, :, None]
        kv_sel = c_kv[bidx, sel]
        bias = rel_bias[t5_bucket(q_pos[None, :, None] - sel)]
        logits = (jnp.einsum('bthc,btkc->bthk', qb, kv_sel).astype(jnp.float32) * (dc ** -0.5)
                  + jnp.moveaxis(bias, -1, 2).astype(jnp.float32))
        logits = jnp.where(valid[:, :, None, :], logits, -jnp.inf)
        probs = jax.nn.softmax(logits, axis=-1)
        return jnp.einsum('bthk,btkc->bthc', probs.astype(kv_sel.dtype), kv_sel)

    out = lax.map(one_block, (jnp.arange(nb, dtype=jnp.int32), blocks(q), blocks(q_idx), blocks(w_idx)))
    return jnp.moveaxis(out, 0, 1).reshape(B, S, H, dc)


def hybrid_mixer(u, w_in, alpha_w, alpha_b, gla_norm_g, gla_w_out, ckv_g, dsa_w_out, rel_bias, w_out):
    B, S, _ = u.shape
    split_pts = np.cumsum(IN_SIZES)[:-1].tolist()
    (gq, gk, gv, gr, ga, dq, dkv, iq, ik, iw, gates) = jnp.split(u @ w_in, split_pts, axis=-1)

    def heads(t, n):
        return t.reshape(B, S, n, -1).transpose(0, 2, 1, 3)

    log_a = jax.nn.log_sigmoid((ga @ alpha_w + alpha_b).astype(jnp.float32)) / GLA_TAU
    o = gla_chunked(heads(gq, GLA_HEADS) * (GLA_DK ** -0.5), heads(gk, GLA_HEADS),
                    heads(gv, GLA_HEADS), heads(log_a, GLA_HEADS))
    o = rms_norm(o, gla_norm_g.reshape(GLA_HEADS, 1, GLA_DV)).astype(u.dtype)
    o = o.transpose(0, 2, 1, 3).reshape(B, S, GLA_HEADS * GLA_DV) * jax.nn.silu(gr)
    y_gla = o @ gla_w_out

    c_kv = rms_norm(dkv, ckv_g)
    od = dsa_attention(dq.reshape(B, S, DSA_HEADS, DSA_LATENT), c_kv,
                       iq.reshape(B, S, IDX_HEADS, IDX_DIM), ik, iw, rel_bias)
    y_dsa = od.reshape(B, S, DSA_HEADS * DSA_LATENT) @ dsa_w_out

    g_gla, g_dsa = jnp.split(jax.nn.sigmoid(gates), 2, axis=-1)
    return (g_gla * y_gla + g_dsa * y_dsa) @ w_out


def setup_inputs(seed: int = 0) -> dict:
    key = jax.random.key(seed)
    ks = jax.random.split(key, 24)
    L, D, F = DEPTH, D_MODEL, D_FF

    def dense(k, shape, fan_in):
        return jax.random.normal(k, shape, jnp.float32) * (fan_in ** -0.5)

    def gain(k, shape):
        return 1.0 + 0.05 * jax.random.normal(k, shape, jnp.float32)

    return {
        "x": jax.random.normal(ks[0], (BATCH, SEQ, D), jnp.float32),
        "p": jax.random.normal(ks[1], (DEPTH, BATCH, SEQ, PLE_DIM), jnp.float32),
        "ffn1_norm": gain(ks[2], (L, D)),
        "ffn1_w_in": dense(ks[3], (L, D, 2 * F), D),
        "ffn1_w_out": dense(ks[4], (L, F, D), F),
        "mix_norm": gain(ks[5], (L, D)),
        "mix_w_in": dense(ks[6], (L, D, D_IN), D),
        "gla_alpha_w": dense(ks[7], (L, GLA_LOWRANK, GLA_HEADS * GLA_DK), GLA_LOWRANK),
        "gla_alpha_b": 0.1 * jax.random.normal(ks[8], (L, GLA_HEADS * GLA_DK), jnp.float32),
        "gla_out_norm": gain(ks[9], (L, GLA_HEADS * GLA_DV)),
        "gla_w_out": dense(ks[10], (L, GLA_HEADS * GLA_DV, D), GLA_HEADS * GLA_DV),
        "ckv_norm": gain(ks[11], (L, DSA_LATENT)),
        "dsa_w_out": dense(ks[12], (L, DSA_HEADS * DSA_LATENT, D), DSA_HEADS * DSA_LATENT),
        "rel_bias": 0.5 * jax.random.normal(ks[13], (REL_BUCKETS, DSA_HEADS), jnp.float32),
        "mix_w_out": dense(ks[14], (L, D, D), D),
        "ffn2_norm": gain(ks[15], (L, D)),
        "ffn2_w_in": dense(ks[16], (L, D, 2 * F), D),
        "ffn2_w_out": dense(ks[17], (L, F, D), F),
        "ple_norm": gain(ks[18], (L, D)),
        "ple_w_gate": dense(ks[19], (L, D, D), D),
        "ple_w_proj": dense(ks[20], (L, PLE_DIM, D), PLE_DIM),
        "final_norm": gain(ks[21], (D,)),
    }


def reference(x, p, ffn1_norm, ffn1_w_in, ffn1_w_out, mix_norm, mix_w_in, gla_alpha_w, gla_alpha_b,
              gla_out_norm, gla_w_out, ckv_norm, dsa_w_out, rel_bias, mix_w_out, ffn2_norm, ffn2_w_in,
              ffn2_w_out, ple_norm, ple_w_gate, ple_w_proj, final_norm):
    h = x
    for i in range(DEPTH):
        h = h + 0.5 * swiglu(rms_norm(h, ffn1_norm[i]), ffn1_w_in[i], ffn1_w_out[i])
        h = h + hybrid_mixer(rms_norm(h, mix_norm[i]), mix_w_in[i], gla_alpha_w[i], gla_alpha_b[i],
                             gla_out_norm[i], gla_w_out[i], ckv_norm[i], dsa_w_out[i], rel_bias,
                             mix_w_out[i])
        h = h + 0.5 * swiglu(rms_norm(h, ffn2_norm[i]), ffn2_w_in[i], ffn2_w_out[i])
        hn = rms_norm(h, ple_norm[i])
        h = h + jax.nn.sigmoid(hn @ ple_w_gate[i]) * (p[i] @ ple_w_proj[i])
    return rms_norm(h, final_norm)
```

```python
import functools
import math

import numpy as np
import jax
import jax.numpy as jnp
from jax import lax
from jax.experimental import pallas as pl
from jax.experimental.pallas import tpu as pltpu

F32 = jnp.float32
BF16 = jnp.bfloat16
HIGHEST = lax.Precision.HIGHEST

EPS = 1e-6
GLA_HEADS = 4
GLA_DK = 128
GLA_DV = 256
GLA_LOWRANK = 16
GLA_TAU = 16.0
DSA_HEADS = 8
DSA_LATENT = 128
IDX_HEADS = 8
IDX_DIM = 64
TOPK_MAX = 256
REL_BUCKETS = 32
REL_MAX_DIST = 128

LANES = 128
VMEM_LIMIT = 56 * 1024 * 1024

TOKEN_TILE = 512
GLA_CHUNK = 128
GLA_SUB = 16
DSA_TILE = 128

MISC_IK = 0
MISC_GA = IDX_DIM
MISC_IW = IDX_DIM + GLA_LOWRANK


def _nt_dot(a, b):
    return lax.dot_general(a, b, (((1,), (1,)), ((), ())), preferred_element_type=F32)


def _rms(x, g):
    return x * lax.rsqrt(jnp.mean(x * x, axis=-1, keepdims=True) + EPS) * g


def _const_spec(shape):
    nd = len(shape)
    return pl.BlockSpec(shape, lambda *_: (0,) * nd, pipeline_mode=pl.Buffered(1))


def _params(sem):
    return pltpu.CompilerParams(dimension_semantics=sem, vmem_limit_bytes=VMEM_LIMIT)


def _ffn_kernel(h_ref, g_ref, win_ref, wout_ref, o_ref, act_ref, *, fc, nf):
    x = h_ref[...]
    xn = _rms(x, g_ref[...]).astype(BF16)
    for f in range(nf):
        gu = jnp.dot(xn, win_ref[f], preferred_element_type=F32)
        gate = gu[:, :fc]
        act_ref[:, f * fc:(f + 1) * fc] = (gate * jax.nn.sigmoid(gate) * gu[:, fc:]).astype(BF16)
    y = jnp.dot(act_ref[...], wout_ref[...], preferred_element_type=F32)
    o_ref[...] = x + 0.5 * y


def _ffn(h, g, w_in, w_out, tm):
    m, d = h.shape
    ff = w_out.shape[0]
    fc = 256
    nf = ff // fc
    win = jnp.concatenate([w_in[:, :ff].reshape(d, nf, fc), w_in[:, ff:].reshape(d, nf, fc)], axis=-1)
    win = win.transpose(1, 0, 2).astype(BF16)
    return pl.pallas_call(
        functools.partial(_ffn_kernel, fc=fc, nf=nf),
        grid=(m // tm,),
        in_specs=[pl.BlockSpec((tm, d), lambda i: (i, 0)),
                  _const_spec((1, d)),
                  _const_spec((nf, d, 2 * fc)),
                  _const_spec((ff, d))],
        out_specs=pl.BlockSpec((tm, d), lambda i: (i, 0)),
        out_shape=jax.ShapeDtypeStruct((m, d), F32),
        scratch_shapes=[pltpu.VMEM((tm, ff), BF16)],
        compiler_params=_params(("parallel",)),
        name="ffn",
    )(h, g.reshape(1, d), win, w_out.astype(BF16))


_PROJ_OUT = (("gq", GLA_HEADS * GLA_DK, F32), ("gk", GLA_HEADS * GLA_DK, F32),
             ("gv", GLA_HEADS * GLA_DV, BF16), ("gr", GLA_HEADS * GLA_DV, BF16),
             ("dq", DSA_HEADS * DSA_LATENT, BF16), ("iq", IDX_HEADS * LANES, BF16),
             ("gates", None, BF16), ("ckv", DSA_LATENT, BF16), ("misc", LANES, F32))


def _inproj_kernel(h_ref, g_ref, ckvg_ref, w_ref, *out_refs, widths):
    xn = _rms(h_ref[...], g_ref[...]).astype(BF16)
    col = 0
    for (name, _, dt), width, o_ref in zip(_PROJ_OUT, widths, out_refs):
        for c0 in range(0, width, 512):
            cw = min(512, width - c0)
            z = jnp.dot(xn, w_ref[:, col + c0:col + c0 + cw], preferred_element_type=F32)
            if name == "ckv":
                z = _rms(z, ckvg_ref[...])
            o_ref[:, c0:c0 + cw] = z.astype(dt)
        col += width


def _inproj(h, g, w_in, ckv_g, tm):
    m, d = h.shape
    sizes = (GLA_HEADS * GLA_DK, GLA_HEADS * GLA_DK, GLA_HEADS * GLA_DV, GLA_HEADS * GLA_DV, GLA_LOWRANK,
             DSA_HEADS * DSA_LATENT, DSA_LATENT, IDX_HEADS * IDX_DIM, IDX_DIM, IDX_HEADS, 2 * d)
    pts = np.cumsum(sizes)[:-1].tolist()
    gq, gk, gv, gr, ga, dq, dkv, iq, ik, iw, gates = jnp.split(w_in, pts, axis=-1)
    iq = jnp.pad(iq.reshape(d, IDX_HEADS, IDX_DIM), ((0, 0), (0, 0), (0, LANES - IDX_DIM))).reshape(d, -1)
    misc = jnp.pad(jnp.concatenate([ik, ga, iw], axis=-1),
                   ((0, 0), (0, LANES - IDX_DIM - GLA_LOWRANK - IDX_HEADS)))
    w = jnp.concatenate([gq, gk, gv, gr, dq, iq, gates, dkv, misc], axis=-1).astype(BF16)
    widths = tuple(wd if wd is not None else 2 * d for _, wd, _ in _PROJ_OUT)
    nw = w.shape[1]
    outs = pl.pallas_call(
        functools.partial(_inproj_kernel, widths=widths),
        grid=(m // tm,),
        in_specs=[pl.BlockSpec((tm, d), lambda i: (i, 0)),
                  _const_spec((1, d)),
                  _const_spec((1, DSA_LATENT)),
                  _const_spec((d, nw))],
        out_specs=[pl.BlockSpec((tm, wd), lambda i: (i, 0)) for wd in widths],
        out_shape=[jax.ShapeDtypeStruct((m, wd), dt) for wd, (_, _, dt) in zip(widths, _PROJ_OUT)],
        compiler_params=_params(("parallel",)),
        name="inproj",
    )(h, g.reshape(1, d), ckv_g.reshape(1, DSA_LATENT), w)
    return dict(zip([n for n, _, _ in _PROJ_OUT], outs))


def _gla_kernel(q_ref, k_ref, v_ref, gr_ref, misc_ref, aw_ref, ab_ref, ng_ref, o_ref, st_ref, a_ref):
    C, SUB = GLA_CHUNK, GLA_SUB

    @pl.when(pl.program_id(2) == 0)
    def _():
        st_ref[...] = jnp.zeros_like(st_ref)

    x = jnp.dot(misc_ref[0], aw_ref[0], precision=HIGHEST, preferred_element_type=F32) + ab_ref[0]
    la = (jnp.minimum(x, 0.0) - jnp.log1p(jnp.exp(-jnp.abs(x)))) / GLA_TAU
    row = lax.broadcasted_iota(jnp.int32, (C, C), 0)
    lane = lax.broadcasted_iota(jnp.int32, (C, C), 1)
    tri = jnp.where(lane <= row, 1.0, 0.0).astype(F32)
    b = jnp.dot(tri, la, precision=HIGHEST, preferred_element_type=F32)

    k = k_ref[0]
    qs = q_ref[0] * (GLA_DK ** -0.5)
    v = v_ref[0]

    sub_t = lax.broadcasted_iota(jnp.int32, (SUB, 1), 0)
    out_lane = lax.broadcasted_iota(jnp.int32, (SUB, C), 1)
    for i in range(C // SUB):
        r0 = i * SUB
        b_i = b[r0:r0 + SUB]
        q_i = qs[r0:r0 + SUB]
        if i == 0:
            a_row = jnp.zeros((SUB, C), F32)
        else:
            beta = b[r0 - 1:r0]
            q_t = (q_i * jnp.exp(b_i - beta)).astype(BF16)
            k_t = (k[:r0] * jnp.exp(beta - b[:r0])).astype(BF16)
            k_t = jnp.concatenate([k_t, jnp.zeros((C - r0, GLA_DK), BF16)], axis=0)
            a_row = _nt_dot(q_t, k_t)
        for s in range(SUB):
            e = jnp.exp(jnp.where(sub_t >= s, b_i - b[r0 + s:r0 + s + 1], -jnp.inf))
            col = jnp.sum(q_i * k[r0 + s:r0 + s + 1] * e, axis=-1, keepdims=True)
            a_row = jnp.where(out_lane == r0 + s, col, a_row)
        a_ref[r0:r0 + SUB, :] = a_row

    st = st_ref[...]
    o = _nt_dot((qs * jnp.exp(b)).astype(BF16), st.astype(BF16))
    o = o + jnp.dot(a_ref[...].astype(BF16), v, preferred_element_type=F32)
    o = _rms(o, ng_ref[0])
    gr = gr_ref[0].astype(F32)
    o_ref[0] = (o * (gr * jax.nn.sigmoid(gr))).astype(o_ref.dtype)

    b_last = b[C - 1:C]
    k_dec = (k * jnp.exp(b_last - b)).astype(BF16)
    upd = lax.dot_general(v, k_dec, (((0,), (0,)), ((), ())), preferred_element_type=F32)
    st_ref[...] = st * jnp.exp(b_last) + upd


def _gla(z, alpha_w, alpha_b, norm_g, batch, seq):
    C = GLA_CHUNK
    H, DK, DV = GLA_HEADS, GLA_DK, GLA_DV

    def r3(a):
        return a.reshape(batch, seq, a.shape[-1])

    aw = jnp.zeros((H, LANES, DK), F32).at[:, MISC_GA:MISC_GA + GLA_LOWRANK, :].set(
        alpha_w.reshape(GLA_LOWRANK, H, DK).transpose(1, 0, 2))
    return pl.pallas_call(
        _gla_kernel,
        grid=(batch, H, seq // C),
        in_specs=[pl.BlockSpec((1, C, DK), lambda b, h, c: (b, c, h)),
                  pl.BlockSpec((1, C, DK), lambda b, h, c: (b, c, h)),
                  pl.BlockSpec((1, C, DV), lambda b, h, c: (b, c, h)),
                  pl.BlockSpec((1, C, DV), lambda b, h, c: (b, c, h)),
                  pl.BlockSpec((1, C, LANES), lambda b, h, c: (b, c, 0)),
                  pl.BlockSpec((1, LANES, DK), lambda b, h, c: (h, 0, 0)),
                  pl.BlockSpec((1, 1, DK), lambda b, h, c: (h, 0, 0)),
                  pl.BlockSpec((1, 1, DV), lambda b, h, c: (h, 0, 0))],
        out_specs=pl.BlockSpec((1, C, DV), lambda b, h, c: (b, c, h)),
        out_shape=jax.ShapeDtypeStruct((batch, seq, H * DV), BF16),
        scratch_shapes=[pltpu.VMEM((DV, DK), F32), pltpu.VMEM((C, C), F32)],
        compiler_params=_params(("parallel", "parallel", "arbitrary")),
        name="gla",
    )(r3(z["gq"]), r3(z["gk"]), r3(z["gv"]), r3(z["gr"]), r3(z["misc"]),
      aw, alpha_b.reshape(H, 1, DK), norm_g.reshape(H, 1, DV))


def _t5_bucket(dist):
    max_exact = REL_BUCKETS // 2
    d = jnp.maximum(dist, 0)
    ratio = jnp.maximum(d, 1).astype(F32) / max_exact
    large = max_exact + (jnp.log(ratio) / math.log(REL_MAX_DIST / max_exact)
                         * (REL_BUCKETS - max_exact)).astype(jnp.int32)
    large = jnp.minimum(large, REL_BUCKETS - 1)
    return jnp.where(d < max_exact, d, large)


def _dsa_bias_kernel(rb_ref, bkt_ref, o_ref):
    h = pl.program_id(0)
    bkt = bkt_ref[...]
    acc = jnp.zeros(bkt.shape, F32)
    for j in range(REL_BUCKETS):
        acc = jnp.where(bkt == j, rb_ref[j, h], acc)
    o_ref[0] = acc


def _dsa_bias(rel_bias):
    T = DSA_TILE
    t = jnp.arange(T, dtype=jnp.int32)
    dist = jnp.stack([t[:, None] - t[None, :], T + t[:, None] - t[None, :]])
    return pl.pallas_call(
        _dsa_bias_kernel,
        grid=(DSA_HEADS,),
        in_specs=[pl.BlockSpec(memory_space=pltpu.SMEM),
                  pl.BlockSpec((2, T, T), lambda h: (0, 0, 0))],
        out_specs=pl.BlockSpec((1, 2, T, T), lambda h: (h, 0, 0, 0)),
        out_shape=jax.ShapeDtypeStruct((DSA_HEADS, 2, T, T), F32),
        name="dsa_bias",
    )(rel_bias, _t5_bucket(dist))


def _dsa_kernel(rb_ref, iq_ref, mq_ref, mk_ref, ckv_ref, dq_ref, bias_ref, o_ref, sc_ref, lg_ref, *, top_k):
    T = DSA_TILE
    qi = pl.program_id(1)
    nk = qi + 1
    neg_inf = -jnp.inf

    iq = iq_ref[0]
    mq = mq_ref[0]
    w_heads = [mq[:, MISC_IW + h:MISC_IW + h + 1] * (IDX_HEADS ** -0.5) * (IDX_DIM ** -0.5)
               for h in range(IDX_HEADS)]
    t_pos = lax.broadcasted_iota(jnp.int32, (T, T), 0)
    s_pos = lax.broadcasted_iota(jnp.int32, (T, T), 1)

    def score_tile(kt, carry):
        keys = mk_ref[0, pl.ds(pl.multiple_of(kt * T, T), T), :].astype(BF16)
        acc = jnp.zeros((T, T), F32)
        for h in range(IDX_HEADS):
            d = _nt_dot(iq[:, h * LANES:(h + 1) * LANES], keys)
            acc = acc + jnp.maximum(d, 0.0) * w_heads[h]
        acc = jnp.where((kt < qi) | (s_pos <= t_pos), acc, neg_inf)
        sc_ref[kt] = acc
        return carry

    lax.fori_loop(0, nk, score_tile, 0)

    def count_ge(cand):
        cb = jnp.broadcast_to(cand, (T, T))

        def body(kt, acc):
            return acc + jnp.where(sc_ref[kt] >= cb, 1.0, 0.0)

        return jnp.sum(lax.fori_loop(0, nk, body, jnp.zeros((T, T), F32)), axis=-1, keepdims=True)

    kf = float(top_k)
    nonneg = count_ge(jnp.zeros((T, 1), F32)) >= kf
    inf_bits = 0x7F800000

    def cand_of(bits):
        return jnp.where(nonneg, pltpu.bitcast(bits, F32), -pltpu.bitcast(inf_bits - bits, F32))

    def bisect(_, lohi):
        lo, hi = lohi
        mid = lo + ((hi - lo) >> 1)
        ok = count_ge(cand_of(mid)) >= kf
        return jnp.where(ok, mid, lo), jnp.where(ok, hi, mid)

    lo, _ = lax.fori_loop(0, 31, bisect, (jnp.zeros((T, 1), jnp.int32),
                                          jnp.full((T, 1), inf_bits + 1, jnp.int32)))
    thr = cand_of(lo)
    thr_b = jnp.broadcast_to(thr, (T, T))

    def count_gt(kt, acc):
        return acc + jnp.where(sc_ref[kt] > thr_b, 1.0, 0.0)

    n_gt = jnp.sum(lax.fori_loop(0, nk, count_gt, jnp.zeros((T, T), F32)), axis=-1, keepdims=True)
    need = kf - n_gt

    incl = jnp.where(t_pos <= s_pos, 1.0, 0.0).astype(BF16)

    def mask_tile(kt, seen):
        sc = sc_ref[kt]
        eq = jnp.where((sc == thr_b) & (sc > neg_inf), 1.0, 0.0)
        rank = seen + jnp.dot(eq.astype(BF16), incl, preferred_element_type=F32)
        keep = (sc > thr_b) | ((eq > 0.0) & (rank <= need))
        sc_ref[kt] = jnp.where(keep, 0.0, neg_inf)
        return seen + jnp.sum(eq, axis=-1, keepdims=True)

    lax.fori_loop(0, nk, mask_tile, jnp.zeros((T, 1), F32))

    scale = DSA_LATENT ** -0.5
    for h in range(DSA_HEADS):
        qh = dq_ref[0, :, h * LANES:(h + 1) * LANES]
        far_bias = rb_ref[REL_BUCKETS - 1, h]

        def logits_tile(kt, m_acc):
            kv = ckv_ref[0, pl.ds(pl.multiple_of(kt * T, T), T), :]
            near = bias_ref[h, jnp.minimum(qi - kt, 1)]
            bias = jnp.where(qi - kt >= 2, far_bias, near)
            lg = _nt_dot(qh, kv) * scale + bias + sc_ref[kt]
            lg_ref[kt] = lg
            return jnp.maximum(m_acc, lg)

        m = jnp.max(lax.fori_loop(0, nk, logits_tile, jnp.full((T, T), neg_inf, F32)), axis=-1, keepdims=True)
        m_b = jnp.broadcast_to(m, (T, T))

        def pv_tile(kt, carry):
            s_acc, acc = carry
            kv = ckv_ref[0, pl.ds(pl.multiple_of(kt * T, T), T), :]
            p = jnp.exp(lg_ref[kt] - m_b)
            return s_acc + p, acc + jnp.dot(p.astype(BF16), kv, preferred_element_type=F32)

        s_acc, acc = lax.fori_loop(0, nk, pv_tile, (jnp.zeros((T, T), F32), jnp.zeros((T, DSA_LATENT), F32)))
        out = acc / jnp.sum(s_acc, axis=-1, keepdims=True)
        o_ref[0, :, h * LANES:(h + 1) * LANES] = out.astype(o_ref.dtype)


def _dsa(z, rel_bias, batch, seq):
    T = DSA_TILE
    nq = seq // T
    top_k = min(TOPK_MAX, seq // 4)
    assert T + 1 >= 16 * 8 ** (15 / 16) + 1 and REL_MAX_DIST <= T

    def r3(a):
        return a.reshape(batch, seq, a.shape[-1])

    misc = r3(z["misc"])
    return pl.pallas_call(
        functools.partial(_dsa_kernel, top_k=top_k),
        grid=(batch, nq),
        in_specs=[pl.BlockSpec(memory_space=pltpu.SMEM),
                  pl.BlockSpec((1, T, IDX_HEADS * LANES), lambda b, q: (b, q, 0)),
                  pl.BlockSpec((1, T, LANES), lambda b, q: (b, q, 0)),
                  pl.BlockSpec((1, seq, LANES), lambda b, q: (b, 0, 0)),
                  pl.BlockSpec((1, seq, DSA_LATENT), lambda b, q: (b, 0, 0)),
                  pl.BlockSpec((1, T, DSA_HEADS * DSA_LATENT), lambda b, q: (b, q, 0)),
                  _const_spec((DSA_HEADS, 2, T, T))],
        out_specs=pl.BlockSpec((1, T, DSA_HEADS * DSA_LATENT), lambda b, q: (b, q, 0)),
        out_shape=jax.ShapeDtypeStruct((batch, seq, DSA_HEADS * DSA_LATENT), BF16),
        scratch_shapes=[pltpu.VMEM((nq, T, T), F32), pltpu.VMEM((nq, T, T), F32)],
        compiler_params=_params(("parallel", "arbitrary")),
        name="dsa",
    )(rel_bias, r3(z["iq"]), misc, misc, r3(z["ckv"]), r3(z["dq"]), _dsa_bias(rel_bias))


def _merge_kernel(h_ref, og_ref, od_ref, gt_ref, wa_ref, wd_ref, wo_ref, o_ref):
    d = h_ref.shape[-1]
    ya = jnp.dot(og_ref[...], wa_ref[...], preferred_element_type=F32)
    yd = jnp.dot(od_ref[...], wd_ref[...], preferred_element_type=F32)
    gt = gt_ref[...].astype(F32)
    mix = jax.nn.sigmoid(gt[:, :d]) * ya + jax.nn.sigmoid(gt[:, d:]) * yd
    o_ref[...] = h_ref[...] + jnp.dot(mix.astype(BF16), wo_ref[...], preferred_element_type=F32)


def _merge(h, og, od, gates, wa, wd, wo, tm):
    m, d = h.shape
    row = lambda w: pl.BlockSpec((tm, w), lambda i: (i, 0))
    return pl.pallas_call(
        _merge_kernel,
        grid=(m // tm,),
        in_specs=[row(d), row(og.shape[1]), row(od.shape[1]), row(2 * d),
                  _const_spec(wa.shape), _const_spec(wd.shape), _const_spec(wo.shape)],
        out_specs=row(d),
        out_shape=jax.ShapeDtypeStruct((m, d), F32),
        compiler_params=_params(("parallel",)),
        name="merge",
    )(h, og, od, gates, wa.astype(BF16), wd.astype(BF16), wo.astype(BF16))


def _ple_final_kernel(h_ref, p_ref, g_ref, wg_ref, wp_ref, fg_ref, o_ref):
    h = h_ref[...]
    hn = _rms(h, g_ref[...]).astype(BF16)
    gate = jax.nn.sigmoid(jnp.dot(hn, wg_ref[...], preferred_element_type=F32))
    emb = jnp.dot(p_ref[...].astype(BF16), wp_ref[...], preferred_element_type=F32)
    o_ref[...] = _rms(h + gate * emb, fg_ref[...])


def _ple_final(h, p, g, wg, wp, fg, tm):
    m, d = h.shape
    pd = p.shape[1]
    row = lambda w: pl.BlockSpec((tm, w), lambda i: (i, 0))
    return pl.pallas_call(
        _ple_final_kernel,
        grid=(m // tm,),
        in_specs=[row(d), row(pd), _const_spec((1, d)), _const_spec(wg.shape), _const_spec(wp.shape),
                  _const_spec((1, d))],
        out_specs=row(d),
        out_shape=jax.ShapeDtypeStruct((m, d), F32),
        compiler_params=_params(("parallel",)),
        name="ple_final",
    )(h, p, g.reshape(1, d), wg.astype(BF16), wp.astype(BF16), fg.reshape(1, d))


def kernel(x, p, ffn1_norm, ffn1_w_in, ffn1_w_out, mix_norm, mix_w_in, gla_alpha_w, gla_alpha_b, gla_out_norm, gla_w_out, ckv_norm, dsa_w_out, rel_bias, mix_w_out, ffn2_norm, ffn2_w_in, ffn2_w_out, ple_norm, ple_w_gate, ple_w_proj, final_norm):
    batch, seq, d = x.shape
    assert p.shape[0] == 1, "single-layer trunk"
    m = batch * seq
    tm = min(TOKEN_TILE, m)
    h = x.reshape(m, d)
    h = _ffn(h, ffn1_norm[0], ffn1_w_in[0], ffn1_w_out[0], tm)
    z = _inproj(h, mix_norm[0], mix_w_in[0], ckv_norm[0], tm)
    og = _gla(z, gla_alpha_w[0], gla_alpha_b[0], gla_out_norm[0], batch, seq)
    od = _dsa(z, rel_bias, batch, seq)
    h = _merge(h, og.reshape(m, -1), od.reshape(m, -1), z["gates"], gla_w_out[0], dsa_w_out[0], mix_w_out[0], tm)
    h = _ffn(h, ffn2_norm[0], ffn2_w_in[0], ffn2_w_out[0], tm)
    h = _ple_final(h, p[0].reshape(m, -1), ple_norm[0], ple_w_gate[0], ple_w_proj[0], final_norm, tm)
    return h.reshape(batch, seq, d)
```

```python
import functools
import math

import numpy as np
import jax
import jax.numpy as jnp
from jax import lax
from jax.experimental import pallas as pl
from jax.experimental.pallas import tpu as pltpu

F32 = jnp.float32
BF16 = jnp.bfloat16
HIGHEST = lax.Precision.HIGHEST

EPS = 1e-6
GLA_HEADS = 4
GLA_DK = 128
GLA_DV = 256
GLA_LOWRANK = 16
GLA_TAU = 16.0
DSA_HEADS = 8
DSA_LATENT = 128
IDX_HEADS = 8
IDX_DIM = 64
TOPK_MAX = 256
REL_BUCKETS = 32
REL_MAX_DIST = 128

LANES = 128
SUBLANES = 8
VMEM_LIMIT = 56 * 1024 * 1024
F32_INF_BITS = 0x7F800000
F32_MAX = float(np.finfo(np.float32).max)

TOKEN_TILE = 512
PROJ_COLS = 512
FFN_COLS = 256
GLA_CHUNK = 128
GLA_SUB = 16
DSA_TQ = 256
DSA_TK = 128

MISC_IK = 0
MISC_GA = IDX_DIM
MISC_IW = IDX_DIM + GLA_LOWRANK


def _nt_dot(a, b):
    return lax.dot_general(a, b, (((1,), (1,)), ((), ())), preferred_element_type=F32)


def _tn_dot(a, b):
    return lax.dot_general(a, b, (((0,), (0,)), ((), ())), preferred_element_type=F32)


def _rms(x, g):
    return x * lax.rsqrt(jnp.mean(x * x, axis=-1, keepdims=True) + EPS) * g


def _const_spec(shape):
    nd = len(shape)
    return pl.BlockSpec(shape, lambda *_: (0,) * nd, pipeline_mode=pl.Buffered(1))


def _params(sem):
    return pltpu.CompilerParams(dimension_semantics=sem, vmem_limit_bytes=VMEM_LIMIT)


def _ffn_kernel(h_ref, g_ref, win_ref, wout_ref, o_ref, act_ref, *, fc, nf):
    x = h_ref[...]
    xn = _rms(x, g_ref[...]).astype(BF16)
    for f in range(nf):
        gu = jnp.dot(xn, win_ref[f], preferred_element_type=F32)
        gate = gu[:, :fc]
        act_ref[:, f * fc:(f + 1) * fc] = (gate * jax.nn.sigmoid(gate) * gu[:, fc:]).astype(BF16)
    y = jnp.dot(act_ref[...], wout_ref[...], preferred_element_type=F32)
    o_ref[...] = x + 0.5 * y


def _ffn(h, g, w_in, w_out, tm):
    m, d = h.shape
    ff = w_out.shape[0]
    fc = FFN_COLS
    nf = ff // fc
    win = jnp.concatenate([w_in[:, :ff].reshape(d, nf, fc), w_in[:, ff:].reshape(d, nf, fc)], axis=-1)
    win = win.transpose(1, 0, 2).astype(BF16)
    return pl.pallas_call(
        functools.partial(_ffn_kernel, fc=fc, nf=nf),
        grid=(m // tm,),
        in_specs=[pl.BlockSpec((tm, d), lambda i: (i, 0)),
                  _const_spec((1, d)),
                  _const_spec((nf, d, 2 * fc)),
                  _const_spec((ff, d))],
        out_specs=pl.BlockSpec((tm, d), lambda i: (i, 0)),
        out_shape=jax.ShapeDtypeStruct((m, d), F32),
        scratch_shapes=[pltpu.VMEM((tm, ff), BF16)],
        compiler_params=_params(("parallel",)),
        name="ffn",
    )(h, g.reshape(1, d), win, w_out.astype(BF16))


_PROJ_OUT = (("gq", GLA_HEADS * GLA_DK, F32), ("gk", GLA_HEADS * GLA_DK, F32),
             ("gv", GLA_HEADS * GLA_DV, BF16), ("gr", GLA_HEADS * GLA_DV, BF16),
             ("dq", DSA_HEADS * DSA_LATENT, BF16), ("iq", IDX_HEADS * LANES, BF16),
             ("gates", None, BF16), ("ckv", DSA_LATENT, BF16), ("misc", LANES, F32))


def _inproj_kernel(h_ref, g_ref, ckvg_ref, w_ref, *out_refs, widths):
    xn = _rms(h_ref[...], g_ref[...]).astype(BF16)
    col = 0
    for (name, _, dt), width, o_ref in zip(_PROJ_OUT, widths, out_refs):
        for c0 in range(0, width, PROJ_COLS):
            cw = min(PROJ_COLS, width - c0)
            z = jnp.dot(xn, w_ref[:, col + c0:col + c0 + cw], preferred_element_type=F32)
            if name == "ckv":
                z = _rms(z, ckvg_ref[...])
            elif name == "dq":
                z = z * (DSA_LATENT ** -0.5)
            o_ref[:, c0:c0 + cw] = z.astype(dt)
        col += width


def _inproj(h, g, w_in, ckv_g, tm):
    m, d = h.shape
    sizes = (GLA_HEADS * GLA_DK, GLA_HEADS * GLA_DK, GLA_HEADS * GLA_DV, GLA_HEADS * GLA_DV, GLA_LOWRANK,
             DSA_HEADS * DSA_LATENT, DSA_LATENT, IDX_HEADS * IDX_DIM, IDX_DIM, IDX_HEADS, 2 * d)
    pts = np.cumsum(sizes)[:-1].tolist()
    gq, gk, gv, gr, ga, dq, dkv, iq, ik, iw, gates = jnp.split(w_in, pts, axis=-1)
    iq = jnp.pad(iq.reshape(d, IDX_HEADS, IDX_DIM), ((0, 0), (0, 0), (0, LANES - IDX_DIM))).reshape(d, -1)
    misc = jnp.pad(jnp.concatenate([ik, ga, iw], axis=-1),
                   ((0, 0), (0, LANES - IDX_DIM - GLA_LOWRANK - IDX_HEADS)))
    w = jnp.concatenate([gq, gk, gv, gr, dq, iq, gates, dkv, misc], axis=-1).astype(BF16)
    widths = tuple(wd if wd is not None else 2 * d for _, wd, _ in _PROJ_OUT)
    nw = w.shape[1]
    outs = pl.pallas_call(
        functools.partial(_inproj_kernel, widths=widths),
        grid=(m // tm,),
        in_specs=[pl.BlockSpec((tm, d), lambda i: (i, 0)),
                  _const_spec((1, d)),
                  _const_spec((1, DSA_LATENT)),
                  _const_spec((d, nw))],
        out_specs=[pl.BlockSpec((tm, wd), lambda i: (i, 0)) for wd in widths],
        out_shape=[jax.ShapeDtypeStruct((m, wd), dt) for wd, (_, _, dt) in zip(widths, _PROJ_OUT)],
        compiler_params=_params(("parallel",)),
        name="inproj",
    )(h, g.reshape(1, d), ckv_g.reshape(1, DSA_LATENT), w)
    return dict(zip([n for n, _, _ in _PROJ_OUT], outs))


def _gla_kernel(q_ref, k_ref, v_ref, gr_ref, misc_ref, aw_ref, ab_ref, ng_ref, o_ref, st_ref, a_ref):
    C, SUB = GLA_CHUNK, GLA_SUB

    @pl.when(pl.program_id(2) == 0)
    def _():
        st_ref[...] = jnp.zeros_like(st_ref)

    x = jnp.dot(misc_ref[0], aw_ref[0], precision=HIGHEST, preferred_element_type=F32) + ab_ref[0]
    la = (jnp.minimum(x, 0.0) - jnp.log1p(jnp.exp(-jnp.abs(x)))) / GLA_TAU
    row = lax.broadcasted_iota(jnp.int32, (C, C), 0)
    lane = lax.broadcasted_iota(jnp.int32, (C, C), 1)
    tri = jnp.where(lane <= row, 1.0, 0.0).astype(F32)
    b = jnp.dot(tri, la, precision=HIGHEST, preferred_element_type=F32)

    k = k_ref[0]
    qs = q_ref[0] * (GLA_DK ** -0.5)
    v = v_ref[0]

    sub_t = lax.broadcasted_iota(jnp.int32, (SUB, 1), 0)
    out_lane = lax.broadcasted_iota(jnp.int32, (SUB, C), 1)
    for i in range(C // SUB):
        r0 = i * SUB
        b_i = b[r0:r0 + SUB]
        q_i = qs[r0:r0 + SUB]
        if i == 0:
            a_row = jnp.zeros((SUB, C), F32)
        else:
            beta = b[r0 - 1:r0]
            q_t = (q_i * jnp.exp(b_i - beta)).astype(BF16)
            k_t = (k[:r0] * jnp.exp(beta - b[:r0])).astype(BF16)
            k_t = jnp.concatenate([k_t, jnp.zeros((C - r0, GLA_DK), BF16)], axis=0)
            a_row = _nt_dot(q_t, k_t)
        for s in range(SUB):
            e = jnp.exp(jnp.where(sub_t >= s, b_i - b[r0 + s:r0 + s + 1], -jnp.inf))
            col = jnp.sum(q_i * k[r0 + s:r0 + s + 1] * e, axis=-1, keepdims=True)
            a_row = jnp.where(out_lane == r0 + s, col, a_row)
        a_ref[r0:r0 + SUB, :] = a_row

    st = st_ref[...]
    o = _nt_dot((qs * jnp.exp(b)).astype(BF16), st.astype(BF16))
    o = o + jnp.dot(a_ref[...].astype(BF16), v, preferred_element_type=F32)
    o = _rms(o, ng_ref[0])
    gr = gr_ref[0].astype(F32)
    o_ref[0] = (o * (gr * jax.nn.sigmoid(gr))).astype(o_ref.dtype)

    b_last = b[C - 1:C]
    k_dec = (k * jnp.exp(b_last - b)).astype(BF16)
    st_ref[...] = st * jnp.exp(b_last) + _tn_dot(v, k_dec)


def _gla(z, alpha_w, alpha_b, norm_g, batch, seq):
    C = GLA_CHUNK
    H, DK, DV = GLA_HEADS, GLA_DK, GLA_DV

    def r3(a):
        return a.reshape(batch, seq, a.shape[-1])

    aw = jnp.zeros((H, LANES, DK), F32).at[:, MISC_GA:MISC_GA + GLA_LOWRANK, :].set(
        alpha_w.reshape(GLA_LOWRANK, H, DK).transpose(1, 0, 2))
    return pl.pallas_call(
        _gla_kernel,
        grid=(batch, H, seq // C),
        in_specs=[pl.BlockSpec((1, C, DK), lambda b, h, c: (b, c, h)),
                  pl.BlockSpec((1, C, DK), lambda b, h, c: (b, c, h)),
                  pl.BlockSpec((1, C, DV), lambda b, h, c: (b, c, h)),
                  pl.BlockSpec((1, C, DV), lambda b, h, c: (b, c, h)),
                  pl.BlockSpec((1, C, LANES), lambda b, h, c: (b, c, 0)),
                  pl.BlockSpec((1, LANES, DK), lambda b, h, c: (h, 0, 0)),
                  pl.BlockSpec((1, 1, DK), lambda b, h, c: (h, 0, 0)),
                  pl.BlockSpec((1, 1, DV), lambda b, h, c: (h, 0, 0))],
        out_specs=pl.BlockSpec((1, C, DV), lambda b, h, c: (b, c, h)),
        out_shape=jax.ShapeDtypeStruct((batch, seq, H * DV), BF16),
        scratch_shapes=[pltpu.VMEM((DV, DK), F32), pltpu.VMEM((C, C), F32)],
        compiler_params=_params(("parallel", "parallel", "arbitrary")),
        name="gla",
    )(r3(z["gq"]), r3(z["gk"]), r3(z["gv"]), r3(z["gr"]), r3(z["misc"]),
      aw, alpha_b.reshape(H, 1, DK), norm_g.reshape(H, 1, DV))


DSA_NEAR = DSA_TQ // DSA_TK + 1


def _t5_bucket(dist):
    max_exact = REL_BUCKETS // 2
    d = jnp.maximum(dist, 0)
    ratio = jnp.maximum(d, 1).astype(F32) / max_exact
    large = max_exact + (jnp.log(ratio) / math.log(REL_MAX_DIST / max_exact)
                         * (REL_BUCKETS - max_exact)).astype(jnp.int32)
    large = jnp.minimum(large, REL_BUCKETS - 1)
    return jnp.where(d < max_exact, d, large)


def _dsa_bias_kernel(rb_ref, bkt_ref, o_ref):
    h = pl.program_id(0)
    bkt = bkt_ref[...]
    far = rb_ref[REL_BUCKETS - 1, h]
    acc = jnp.zeros(bkt.shape, F32)
    for j in range(REL_BUCKETS):
        acc = jnp.where(bkt == j, rb_ref[j, h] - far, acc)
    o_ref[0] = acc


def _dsa_bias(rel_bias):
    assert DSA_TK + 1 >= REL_MAX_DIST
    s = jnp.arange(DSA_TK, dtype=jnp.int32)[:, None]
    t = jnp.arange(DSA_TQ, dtype=jnp.int32)[None, :]
    near = [_t5_bucket(t - s + DSA_TK * (1 - j)) for j in range(DSA_NEAR)]
    bkt = jnp.stack([jnp.full((DSA_TK, DSA_TQ), -1, jnp.int32)] + near)
    n = DSA_NEAR + 1
    return pl.pallas_call(
        _dsa_bias_kernel,
        grid=(DSA_HEADS,),
        in_specs=[pl.BlockSpec(memory_space=pltpu.SMEM),
                  pl.BlockSpec((n, DSA_TK, DSA_TQ), lambda h: (0, 0, 0))],
        out_specs=pl.BlockSpec((1, n, DSA_TK, DSA_TQ), lambda h: (h, 0, 0, 0)),
        out_shape=jax.ShapeDtypeStruct((DSA_HEADS, n, DSA_TK, DSA_TQ), F32),
        name="dsa_bias",
    )(rel_bias, bkt)


def _dsa_kernel(iq_ref, mq_ref, mk_ref, ckv_ref, dq_ref, bias_ref, o_ref,
                sc_ref, lg_ref, iqt_ref, dqt_ref, wt_ref, acc_ref, *, top_k):
    TQ, TK = DSA_TQ, DSA_TK
    R = TQ // TK
    H = DSA_HEADS
    qi = pl.program_id(1)
    ngroups = qi + 1
    neg_inf = -jnp.inf
    kf = float(top_k)

    wt_ref[...] = mq_ref[0].T * ((IDX_HEADS ** -0.5) * (IDX_DIM ** -0.5))
    for h in range(H):
        iqt_ref[h] = iq_ref[0, :, h * LANES:(h + 1) * LANES].T
        dqt_ref[h] = dq_ref[0, :, h * LANES:(h + 1) * LANES].T

    def key_rows(ref, kt):
        return ref[0, pl.ds(pl.multiple_of(kt * TK, TK), TK), :]

    k_pos = lax.broadcasted_iota(jnp.int32, (TK, TQ), 0)
    q_pos = lax.broadcasted_iota(jnp.int32, (TK, TQ), 1) + qi * TQ

    def score_group(g, carry):
        for u in range(R):
            kt = g * R + u
            keys = key_rows(mk_ref, kt).astype(BF16)
            acc = jnp.zeros((TK, TQ), F32)
            for h in range(IDX_HEADS):
                d = jnp.dot(keys, iqt_ref[h], preferred_element_type=F32)
                acc = acc + jnp.maximum(d, 0.0) * wt_ref[MISC_IW + h:MISC_IW + h + 1, :]
            sc_ref[kt] = jnp.where(k_pos + kt * TK <= q_pos, acc, neg_inf)
        return carry

    lax.fori_loop(0, ngroups, score_group, 0)

    def count(pred, cand):
        def body(g, acc):
            for u in range(R):
                acc = acc + jnp.where(pred(sc_ref[g * R + u], cand), 1.0, 0.0)
            return acc

        acc = lax.fori_loop(0, ngroups, body, jnp.zeros((TK, TQ), F32))
        return jnp.sum(acc, axis=0, keepdims=True)

    ge = lambda a, c: a >= c
    nonneg = count(ge, jnp.zeros((1, TQ), F32)) >= kf

    def cand_of(bits):
        return jnp.where(nonneg, pltpu.bitcast(bits, F32), -pltpu.bitcast(F32_INF_BITS - bits, F32))

    def bisect(_, lohi):
        lo, hi = lohi
        mid = lo + ((hi - lo) >> 1)
        ok = count(ge, cand_of(mid)) >= kf
        return jnp.where(ok, mid, lo), jnp.where(ok, hi, mid)

    lo, _ = lax.fori_loop(0, 31, bisect, (jnp.zeros((1, TQ), jnp.int32),
                                          jnp.full((1, TQ), F32_INF_BITS + 1, jnp.int32)))
    thr = jnp.maximum(cand_of(lo), -F32_MAX)
    n_ge = count(ge, thr)
    has_ties = jnp.max(n_ge) > kf

    @pl.when(jnp.logical_not(has_ties))
    def _():
        def body(g, carry):
            for u in range(R):
                kt = g * R + u
                sc_ref[kt] = jnp.where(sc_ref[kt] >= thr, 0.0, neg_inf)
            return carry

        lax.fori_loop(0, ngroups, body, 0)

    @pl.when(has_ties)
    def _():
        need = kf - count(lambda a, c: a > c, thr)
        incl = jnp.where(lax.broadcasted_iota(jnp.int32, (TK, TK), 1)
                         <= lax.broadcasted_iota(jnp.int32, (TK, TK), 0), 1.0, 0.0).astype(BF16)

        def body(g, seen):
            for u in range(R):
                kt = g * R + u
                sc = sc_ref[kt]
                eq = jnp.where(sc == thr, 1.0, 0.0)
                rank = seen + jnp.dot(incl, eq.astype(BF16), preferred_element_type=F32)
                keep = jnp.where(sc > thr, 1.0, jnp.where(rank <= need, eq, 0.0))
                sc_ref[kt] = jnp.where(keep > 0.0, 0.0, neg_inf)
                seen = seen + jnp.sum(eq, axis=0, keepdims=True)
            return seen

        lax.fori_loop(0, ngroups, body, jnp.zeros((1, TQ), F32))

    def fold(x, op):
        return op(x.reshape(TK // SUBLANES, SUBLANES, TQ), axis=0)

    def logits_group(g, m_acc):
        m_acc = list(m_acc)
        for u in range(R):
            kt = g * R + u
            kv = key_rows(ckv_ref, kt)
            near = jnp.clip(kt - (R * qi - 1), -1, DSA_NEAR - 1) + 1
            mask = sc_ref[kt]
            for h in range(H):
                lg = jnp.dot(kv, dqt_ref[h], preferred_element_type=F32) + (mask + bias_ref[h, near])
                lg_ref[h, kt] = lg
                m_acc[h] = jnp.maximum(m_acc[h], fold(lg, jnp.max))
        return tuple(m_acc)

    m_acc = lax.fori_loop(0, ngroups, logits_group, (jnp.full((SUBLANES, TQ), neg_inf, F32),) * H)
    m_row = [jnp.max(m, axis=0, keepdims=True) for m in m_acc]

    acc_ref[...] = jnp.zeros_like(acc_ref)

    def pv_group(g, l_acc):
        l_acc = list(l_acc)
        for u in range(R):
            kt = g * R + u
            kv = key_rows(ckv_ref, kt)
            for h in range(H):
                p = jnp.exp(lg_ref[h, kt] - m_row[h])
                l_acc[h] = l_acc[h] + fold(p, jnp.sum)
                acc_ref[h] += _tn_dot(kv, p.astype(BF16))
        return tuple(l_acc)

    l_acc = lax.fori_loop(0, ngroups, pv_group, (jnp.zeros((SUBLANES, TQ), F32),) * H)
    for h in range(H):
        out_t = acc_ref[h] / jnp.sum(l_acc[h], axis=0, keepdims=True)
        o_ref[0, :, h * LANES:(h + 1) * LANES] = out_t.T.astype(o_ref.dtype)


def _dsa(z, rel_bias, batch, seq):
    TQ, TK = DSA_TQ, DSA_TK
    nq, nk = seq // TQ, seq // TK
    top_k = min(TOPK_MAX, seq // 4)
    H = DSA_HEADS

    def r3(a):
        return a.reshape(batch, seq, a.shape[-1])

    misc = r3(z["misc"])
    return pl.pallas_call(
        functools.partial(_dsa_kernel, top_k=top_k),
        grid=(batch, nq),
        in_specs=[pl.BlockSpec((1, TQ, IDX_HEADS * LANES), lambda b, q: (b, q, 0)),
                  pl.BlockSpec((1, TQ, LANES), lambda b, q: (b, q, 0)),
                  pl.BlockSpec((1, seq, LANES), lambda b, q: (b, 0, 0)),
                  pl.BlockSpec((1, seq, DSA_LATENT), lambda b, q: (b, 0, 0)),
                  pl.BlockSpec((1, TQ, H * DSA_LATENT), lambda b, q: (b, q, 0)),
                  _const_spec((H, DSA_NEAR + 1, TK, TQ))],
        out_specs=pl.BlockSpec((1, TQ, H * DSA_LATENT), lambda b, q: (b, q, 0)),
        out_shape=jax.ShapeDtypeStruct((batch, seq, H * DSA_LATENT), BF16),
        scratch_shapes=[pltpu.VMEM((nk, TK, TQ), F32),
                        pltpu.VMEM((H, nk, TK, TQ), F32),
                        pltpu.VMEM((IDX_HEADS, LANES, TQ), BF16),
                        pltpu.VMEM((H, DSA_LATENT, TQ), BF16),
                        pltpu.VMEM((LANES, TQ), F32),
                        pltpu.VMEM((H, DSA_LATENT, TQ), F32)],
        compiler_params=_params(("parallel", "arbitrary")),
        name="dsa",
    )(r3(z["iq"]), misc, misc, r3(z["ckv"]), r3(z["dq"]), _dsa_bias(rel_bias))


def _merge_kernel(h_ref, og_ref, od_ref, gt_ref, wa_ref, wd_ref, wo_ref, o_ref):
    d = h_ref.shape[-1]
    ya = jnp.dot(og_ref[...], wa_ref[...], preferred_element_type=F32)
    yd = jnp.dot(od_ref[...], wd_ref[...], preferred_element_type=F32)
    gt = gt_ref[...].astype(F32)
    mix = jax.nn.sigmoid(gt[:, :d]) * ya + jax.nn.sigmoid(gt[:, d:]) * yd
    o_ref[...] = h_ref[...] + jnp.dot(mix.astype(BF16), wo_ref[...], preferred_element_type=F32)


def _merge(h, og, od, gates, wa, wd, wo, tm):
    m, d = h.shape
    row = lambda w: pl.BlockSpec((tm, w), lambda i: (i, 0))
    return pl.pallas_call(
        _merge_kernel,
        grid=(m // tm,),
        in_specs=[row(d), row(og.shape[1]), row(od.shape[1]), row(2 * d),
                  _const_spec(wa.shape), _const_spec(wd.shape), _const_spec(wo.shape)],
        out_specs=row(d),
        out_shape=jax.ShapeDtypeStruct((m, d), F32),
        compiler_params=_params(("parallel",)),
        name="merge",
    )(h, og, od, gates, wa.astype(BF16), wd.astype(BF16), wo.astype(BF16))


def _ple_final_kernel(h_ref, p_ref, g_ref, wg_ref, wp_ref, fg_ref, o_ref):
    h = h_ref[...]
    hn = _rms(h, g_ref[...]).astype(BF16)
    gate = jax.nn.sigmoid(jnp.dot(hn, wg_ref[...], preferred_element_type=F32))
    emb = jnp.dot(p_ref[...].astype(BF16), wp_ref[...], preferred_element_type=F32)
    o_ref[...] = _rms(h + gate * emb, fg_ref[...])


def _ple_final(h, p, g, wg, wp, fg, tm):
    m, d = h.shape
    pd = p.shape[1]
    row = lambda w: pl.BlockSpec((tm, w), lambda i: (i, 0))
    return pl.pallas_call(
        _ple_final_kernel,
        grid=(m // tm,),
        in_specs=[row(d), row(pd), _const_spec((1, d)), _const_spec(wg.shape), _const_spec(wp.shape),
                  _const_spec((1, d))],
        out_specs=row(d),
        out_shape=jax.ShapeDtypeStruct((m, d), F32),
        compiler_params=_params(("parallel",)),
        name="ple_final",
    )(h, p, g.reshape(1, d), wg.astype(BF16), wp.astype(BF16), fg.reshape(1, d))


def kernel(x, p, ffn1_norm, ffn1_w_in, ffn1_w_out, mix_norm, mix_w_in, gla_alpha_w, gla_alpha_b, gla_out_norm, gla_w_out, ckv_norm, dsa_w_out, rel_bias, mix_w_out, ffn2_norm, ffn2_w_in, ffn2_w_out, ple_norm, ple_w_gate, ple_w_proj, final_norm):
    batch, seq, d = x.shape
    assert p.shape[0] == 1, "single-layer trunk"
    m = batch * seq
    tm = min(TOKEN_TILE, m)
    h = x.reshape(m, d)
    h = _ffn(h, ffn1_norm[0], ffn1_w_in[0], ffn1_w_out[0], tm)
    z = _inproj(h, mix_norm[0], mix_w_in[0], ckv_norm[0], tm)
    og = _gla(z, gla_alpha_w[0], gla_alpha_b[0], gla_out_norm[0], batch, seq)
    od = _dsa(z, rel_bias, batch, seq)
    h = _merge(h, og.reshape(m, -1), od.reshape(m, -1), z["gates"], gla_w_out[0], dsa_w_out[0], mix_w_out[0], tm)
    h = _ffn(h, ffn2_norm[0], ffn2_w_in[0], ffn2_w_out[0], tm)
    h = _ple_final(h, p[0].reshape(m, -1), ple_norm[0], ple_w_gate[0], ple_w_proj[0], final_norm, tm)
    return h.reshape(batch, seq, d)
```

```python
import functools
import math

import numpy as np
import jax
import jax.numpy as jnp
from jax import lax
from jax.experimental import pallas as pl
from jax.experimental.pallas import tpu as pltpu

F32 = jnp.float32
BF16 = jnp.bfloat16
HIGHEST = lax.Precision.HIGHEST

EPS = 1e-6
GLA_HEADS = 4
GLA_DK = 128
GLA_DV = 256
GLA_LOWRANK = 16
GLA_TAU = 16.0
DSA_HEADS = 8
DSA_LATENT = 128
IDX_HEADS = 8
IDX_DIM = 64
TOPK_MAX = 256
REL_BUCKETS = 32
REL_MAX_DIST = 128

LANES = 128
SUBLANES = 8
VMEM_LIMIT = 56 * 1024 * 1024
F32_INF_BITS = 0x7F800000
F32_MAX = float(np.finfo(np.float32).max)

TOKEN_TILE = 512
PROJ_COLS = 512
FFN_COLS = 256
GLA_CHUNK = 128
GLA_SUB = 16
DSA_TQ = 256
DSA_TK = 256

MISC_IK = 0
MISC_GA = IDX_DIM
MISC_IW = IDX_DIM + GLA_LOWRANK


def _nt_dot(a, b):
    return lax.dot_general(a, b, (((1,), (1,)), ((), ())), preferred_element_type=F32)


def _tn_dot(a, b):
    return lax.dot_general(a, b, (((0,), (0,)), ((), ())), preferred_element_type=F32)


def _rms(x, g):
    return x * lax.rsqrt(jnp.mean(x * x, axis=-1, keepdims=True) + EPS) * g


def _const_spec(shape):
    nd = len(shape)
    return pl.BlockSpec(shape, lambda *_: (0,) * nd, pipeline_mode=pl.Buffered(1))


def _params(sem):
    return pltpu.CompilerParams(dimension_semantics=sem, vmem_limit_bytes=VMEM_LIMIT)


def _ffn_kernel(h_ref, g_ref, win_ref, wout_ref, o_ref, act_ref, *, fc, nf):
    x = h_ref[...]
    xn = _rms(x, g_ref[...]).astype(BF16)
    for f in range(nf):
        gu = jnp.dot(xn, win_ref[f], preferred_element_type=F32)
        gate = gu[:, :fc]
        act_ref[:, f * fc:(f + 1) * fc] = (gate * jax.nn.sigmoid(gate) * gu[:, fc:]).astype(BF16)
    y = jnp.dot(act_ref[...], wout_ref[...], preferred_element_type=F32)
    o_ref[...] = x + 0.5 * y


def _ffn(h, g, w_in, w_out, tm):
    m, d = h.shape
    ff = w_out.shape[0]
    fc = FFN_COLS
    nf = ff // fc
    win = jnp.concatenate([w_in[:, :ff].reshape(d, nf, fc), w_in[:, ff:].reshape(d, nf, fc)], axis=-1)
    win = win.transpose(1, 0, 2).astype(BF16)
    return pl.pallas_call(
        functools.partial(_ffn_kernel, fc=fc, nf=nf),
        grid=(m // tm,),
        in_specs=[pl.BlockSpec((tm, d), lambda i: (i, 0)),
                  _const_spec((1, d)),
                  _const_spec((nf, d, 2 * fc)),
                  _const_spec((ff, d))],
        out_specs=pl.BlockSpec((tm, d), lambda i: (i, 0)),
        out_shape=jax.ShapeDtypeStruct((m, d), F32),
        scratch_shapes=[pltpu.VMEM((tm, ff), BF16)],
        compiler_params=_params(("parallel",)),
        name="ffn",
    )(h, g.reshape(1, d), win, w_out.astype(BF16))


_PROJ_OUT = (("gq", GLA_HEADS * GLA_DK, F32), ("gk", GLA_HEADS * GLA_DK, F32),
             ("gv", GLA_HEADS * GLA_DV, BF16), ("gr", GLA_HEADS * GLA_DV, BF16),
             ("dq", DSA_HEADS * DSA_LATENT, BF16), ("iq", IDX_HEADS * LANES, BF16),
             ("gates", None, BF16), ("ckv", DSA_LATENT, BF16), ("misc", LANES, F32))


def _inproj_kernel(h_ref, g_ref, ckvg_ref, w_ref, *out_refs, widths):
    xn = _rms(h_ref[...], g_ref[...]).astype(BF16)
    col = 0
    for (name, _, dt), width, o_ref in zip(_PROJ_OUT, widths, out_refs):
        for c0 in range(0, width, PROJ_COLS):
            cw = min(PROJ_COLS, width - c0)
            z = jnp.dot(xn, w_ref[:, col + c0:col + c0 + cw], preferred_element_type=F32)
            if name == "ckv":
                z = _rms(z, ckvg_ref[...])
            elif name == "dq":
                z = z * (DSA_LATENT ** -0.5)
            o_ref[:, c0:c0 + cw] = z.astype(dt)
        col += width


def _inproj(h, g, w_in, ckv_g, tm):
    m, d = h.shape
    sizes = (GLA_HEADS * GLA_DK, GLA_HEADS * GLA_DK, GLA_HEADS * GLA_DV, GLA_HEADS * GLA_DV, GLA_LOWRANK,
             DSA_HEADS * DSA_LATENT, DSA_LATENT, IDX_HEADS * IDX_DIM, IDX_DIM, IDX_HEADS, 2 * d)
    pts = np.cumsum(sizes)[:-1].tolist()
    gq, gk, gv, gr, ga, dq, dkv, iq, ik, iw, gates = jnp.split(w_in, pts, axis=-1)
    iq = jnp.pad(iq.reshape(d, IDX_HEADS, IDX_DIM), ((0, 0), (0, 0), (0, LANES - IDX_DIM))).reshape(d, -1)
    misc = jnp.pad(jnp.concatenate([ik, ga, iw], axis=-1),
                   ((0, 0), (0, LANES - IDX_DIM - GLA_LOWRANK - IDX_HEADS)))
    w = jnp.concatenate([gq, gk, gv, gr, dq, iq, gates, dkv, misc], axis=-1).astype(BF16)
    widths = tuple(wd if wd is not None else 2 * d for _, wd, _ in _PROJ_OUT)
    nw = w.shape[1]
    outs = pl.pallas_call(
        functools.partial(_inproj_kernel, widths=widths),
        grid=(m // tm,),
        in_specs=[pl.BlockSpec((tm, d), lambda i: (i, 0)),
                  _const_spec((1, d)),
                  _const_spec((1, DSA_LATENT)),
                  _const_spec((d, nw))],
        out_specs=[pl.BlockSpec((tm, wd), lambda i: (i, 0)) for wd in widths],
        out_shape=[jax.ShapeDtypeStruct((m, wd), dt) for wd, (_, _, dt) in zip(widths, _PROJ_OUT)],
        compiler_params=_params(("parallel",)),
        name="inproj",
    )(h, g.reshape(1, d), ckv_g.reshape(1, DSA_LATENT), w)
    return dict(zip([n for n, _, _ in _PROJ_OUT], outs))


def _gla_scores_safe(qs, k, b, a_ref):
    C, SUB = GLA_CHUNK, GLA_SUB
    sub_t = lax.broadcasted_iota(jnp.int32, (SUB, 1), 0)
    out_lane = lax.broadcasted_iota(jnp.int32, (SUB, C), 1)
    for i in range(C // SUB):
        r0 = i * SUB
        b_i = b[r0:r0 + SUB]
        q_i = qs[r0:r0 + SUB]
        if i == 0:
            a_row = jnp.zeros((SUB, C), F32)
        else:
            beta = b[r0 - 1:r0]
            q_t = (q_i * jnp.exp(b_i - beta)).astype(BF16)
            k_t = (k[:r0] * jnp.exp(beta - b[:r0])).astype(BF16)
            k_t = jnp.concatenate([k_t, jnp.zeros((C - r0, GLA_DK), BF16)], axis=0)
            a_row = _nt_dot(q_t, k_t)
        for s in range(SUB):
            e = jnp.exp(jnp.where(sub_t >= s, b_i - b[r0 + s:r0 + s + 1], -jnp.inf))
            col = jnp.sum(q_i * k[r0 + s:r0 + s + 1] * e, axis=-1, keepdims=True)
            a_row = jnp.where(out_lane == r0 + s, col, a_row)
        a_ref[r0:r0 + SUB, :] = a_row


def _gla_kernel(q_ref, k_ref, v_ref, gr_ref, misc_ref, aw_ref, ab_ref, ng_ref, o_ref, st_ref, a_ref):
    C = GLA_CHUNK
    DK, DV = GLA_DK, GLA_DV

    @pl.when(pl.program_id(1) == 0)
    def _():
        st_ref[...] = jnp.zeros_like(st_ref)

    x = jnp.dot(misc_ref[0], aw_ref[...], precision=HIGHEST, preferred_element_type=F32) + ab_ref[...]
    la = (jnp.minimum(x, 0.0) - jnp.log1p(jnp.exp(-jnp.abs(x)))) / GLA_TAU
    row = lax.broadcasted_iota(jnp.int32, (C, C), 0)
    col = lax.broadcasted_iota(jnp.int32, (C, C), 1)
    causal = col <= row
    b_all = jnp.dot(jnp.where(causal, 1.0, 0.0).astype(F32), la, precision=HIGHEST,
                    preferred_element_type=F32)

    for h in range(GLA_HEADS):
        b = b_all[:, h * DK:(h + 1) * DK]
        k = k_ref[0, :, h * DK:(h + 1) * DK]
        qs = q_ref[0, :, h * DK:(h + 1) * DK] * (DK ** -0.5)
        v = v_ref[0, :, h * DV:(h + 1) * DV]
        b_last = b[C - 1:C]
        q_in = qs * jnp.exp(b)
        k_out = k * jnp.exp(-b)

        k_max = jnp.max(jnp.abs(k), axis=0, keepdims=True) * jnp.exp(-b_last)
        q_max = jnp.max(jnp.abs(q_in), axis=0, keepdims=True)
        factored_ok = jnp.sum(jnp.maximum(q_max, 1.0) * k_max) < 1e30

        @pl.when(factored_ok)
        def _():
            a_ref[...] = jnp.where(causal, _nt_dot(q_in.astype(BF16), k_out.astype(BF16)), 0.0)

        @pl.when(jnp.logical_not(factored_ok))
        def _():
            _gla_scores_safe(qs, k, b, a_ref)

        st = st_ref[h]
        o = _nt_dot(q_in.astype(BF16), st.astype(BF16))
        o = o + jnp.dot(a_ref[...].astype(BF16), v, preferred_element_type=F32)
        o = _rms(o, ng_ref[:, h * DV:(h + 1) * DV])
        gr = gr_ref[0, :, h * DV:(h + 1) * DV].astype(F32)
        o_ref[0, :, h * DV:(h + 1) * DV] = (o * (gr * jax.nn.sigmoid(gr))).astype(o_ref.dtype)

        k_dec = (k * jnp.exp(b_last - b)).astype(BF16)
        st_ref[h] = st * jnp.exp(b_last) + _tn_dot(v, k_dec)


def _gla(z, alpha_w, alpha_b, norm_g, batch, seq):
    C = GLA_CHUNK
    H, DK, DV = GLA_HEADS, GLA_DK, GLA_DV

    def r3(a):
        return a.reshape(batch, seq, a.shape[-1])

    aw = jnp.zeros((LANES, H * DK), F32).at[MISC_GA:MISC_GA + GLA_LOWRANK, :].set(alpha_w)
    tok = lambda w: pl.BlockSpec((1, C, w), lambda b, c: (b, c, 0))
    return pl.pallas_call(
        _gla_kernel,
        grid=(batch, seq // C),
        in_specs=[tok(H * DK), tok(H * DK), tok(H * DV), tok(H * DV), tok(LANES),
                  _const_spec((LANES, H * DK)), _const_spec((1, H * DK)), _const_spec((1, H * DV))],
        out_specs=tok(H * DV),
        out_shape=jax.ShapeDtypeStruct((batch, seq, H * DV), BF16),
        scratch_shapes=[pltpu.VMEM((H, DV, DK), F32), pltpu.VMEM((C, C), F32)],
        compiler_params=_params(("parallel", "arbitrary")),
        name="gla",
    )(r3(z["gq"]), r3(z["gk"]), r3(z["gv"]), r3(z["gr"]), r3(z["misc"]),
      aw, alpha_b.reshape(1, H * DK), norm_g.reshape(1, H * DV))


DSA_NEAR = DSA_TQ // DSA_TK + 1


def _t5_bucket(dist):
    max_exact = REL_BUCKETS // 2
    d = jnp.maximum(dist, 0)
    ratio = jnp.maximum(d, 1).astype(F32) / max_exact
    large = max_exact + (jnp.log(ratio) / math.log(REL_MAX_DIST / max_exact)
                         * (REL_BUCKETS - max_exact)).astype(jnp.int32)
    large = jnp.minimum(large, REL_BUCKETS - 1)
    return jnp.where(d < max_exact, d, large)


def _dsa_bias_kernel(rb_ref, bkt_ref, o_ref):
    h = pl.program_id(0)
    bkt = bkt_ref[...]
    far = rb_ref[REL_BUCKETS - 1, h]
    acc = jnp.zeros(bkt.shape, F32)
    for j in range(REL_BUCKETS):
        acc = jnp.where(bkt == j, rb_ref[j, h] - far, acc)
    o_ref[0] = acc


def _dsa_bias(rel_bias):
    assert DSA_TK + 1 >= REL_MAX_DIST
    s = jnp.arange(DSA_TK, dtype=jnp.int32)[:, None]
    t = jnp.arange(DSA_TQ, dtype=jnp.int32)[None, :]
    near = [_t5_bucket(t - s + DSA_TK * (1 - j)) for j in range(DSA_NEAR)]
    bkt = jnp.stack([jnp.full((DSA_TK, DSA_TQ), -1, jnp.int32)] + near)
    n = DSA_NEAR + 1
    return pl.pallas_call(
        _dsa_bias_kernel,
        grid=(DSA_HEADS,),
        in_specs=[pl.BlockSpec(memory_space=pltpu.SMEM),
                  pl.BlockSpec((n, DSA_TK, DSA_TQ), lambda h: (0, 0, 0))],
        out_specs=pl.BlockSpec((1, n, DSA_TK, DSA_TQ), lambda h: (h, 0, 0, 0)),
        out_shape=jax.ShapeDtypeStruct((DSA_HEADS, n, DSA_TK, DSA_TQ), F32),
        name="dsa_bias",
    )(rel_bias, bkt)


def _dsa_kernel(iq_ref, mq_ref, mk_ref, ckv_ref, dq_ref, bias_ref, o_ref,
                sc_ref, lg_ref, iqt_ref, dqt_ref, wt_ref, acc_ref, *, top_k):
    TQ, TK = DSA_TQ, DSA_TK
    R = TQ // TK
    H = DSA_HEADS
    qi = pl.program_id(1)
    ngroups = qi + 1
    neg_inf = -jnp.inf
    kf = float(top_k)

    wt_ref[...] = mq_ref[0].T * ((IDX_HEADS ** -0.5) * (IDX_DIM ** -0.5))
    for h in range(H):
        iqt_ref[h] = iq_ref[0, :, h * LANES:(h + 1) * LANES].T
        dqt_ref[h] = dq_ref[0, :, h * LANES:(h + 1) * LANES].T

    def key_rows(ref, kt):
        return ref[0, pl.ds(pl.multiple_of(kt * TK, TK), TK), :]

    def fold(x, op):
        return op(x.reshape(TK // SUBLANES, SUBLANES, TQ), axis=0)

    k_pos = lax.broadcasted_iota(jnp.int32, (TK, TQ), 0)
    q_pos = lax.broadcasted_iota(jnp.int32, (TK, TQ), 1) + qi * TQ

    def score_group(g, carry):
        for u in range(R):
            kt = g * R + u
            keys = key_rows(mk_ref, kt).astype(BF16)
            acc = jnp.zeros((TK, TQ), F32)
            for h in range(IDX_HEADS):
                d = jnp.dot(keys, iqt_ref[h], preferred_element_type=F32)
                acc = acc + jnp.maximum(d, 0.0) * wt_ref[MISC_IW + h:MISC_IW + h + 1, :]
            sc_ref[kt] = jnp.where(k_pos + kt * TK <= q_pos, acc, neg_inf)
        return carry

    lax.fori_loop(0, ngroups, score_group, 0)

    def count(pred, cand):
        rows = 4 * SUBLANES

        def body(g, acc):
            for u in range(R):
                hit = jnp.where(pred(sc_ref[g * R + u], cand), 1.0, 0.0)
                acc = acc + jnp.sum(hit.reshape(TK // rows, rows, TQ), axis=0)
            return acc

        acc = lax.fori_loop(0, ngroups, body, jnp.zeros((rows, TQ), F32))
        return jnp.sum(acc, axis=0, keepdims=True)

    ge = lambda a, c: a >= c

    nonneg = count(ge, jnp.zeros((1, TQ), F32)) >= kf

    def cand_of(bits):
        return jnp.where(nonneg, pltpu.bitcast(bits, F32), -pltpu.bitcast(F32_INF_BITS - bits, F32))

    def bisect(_, lohi):
        lo, hi = lohi
        mid = lo + ((hi - lo) >> 1)
        ok = count(ge, cand_of(mid)) >= kf
        return jnp.where(ok, mid, lo), jnp.where(ok, hi, mid)

    lo, _ = lax.fori_loop(0, 31, bisect, (jnp.zeros((1, TQ), jnp.int32),
                                          jnp.full((1, TQ), F32_INF_BITS + 1, jnp.int32)))
    thr = jnp.maximum(cand_of(lo), -F32_MAX)
    has_ties = jnp.max(count(ge, thr)) > kf

    @pl.when(jnp.logical_not(has_ties))
    def _():
        def body(g, carry):
            for u in range(R):
                kt = g * R + u
                sc_ref[kt] = jnp.where(sc_ref[kt] >= thr, 0.0, neg_inf)
            return carry

        lax.fori_loop(0, ngroups, body, 0)

    @pl.when(has_ties)
    def _():
        need = kf - count(lambda a, c: a > c, thr)
        incl = jnp.where(lax.broadcasted_iota(jnp.int32, (TK, TK), 1)
                         <= lax.broadcasted_iota(jnp.int32, (TK, TK), 0), 1.0, 0.0).astype(BF16)

        def body(g, seen):
            for u in range(R):
                kt = g * R + u
                sc = sc_ref[kt]
                eq = jnp.where(sc == thr, 1.0, 0.0)
                rank = seen + jnp.dot(incl, eq.astype(BF16), preferred_element_type=F32)
                keep = jnp.where(sc > thr, 1.0, jnp.where(rank <= need, eq, 0.0))
                sc_ref[kt] = jnp.where(keep > 0.0, 0.0, neg_inf)
                seen = seen + jnp.sum(eq, axis=0, keepdims=True)
            return seen

        lax.fori_loop(0, ngroups, body, jnp.zeros((1, TQ), F32))


    def logits_group(g, m_acc):
        m_acc = list(m_acc)
        for u in range(R):
            kt = g * R + u
            kv = key_rows(ckv_ref, kt)
            near = jnp.clip(kt - (R * qi - 1), -1, DSA_NEAR - 1) + 1
            mask = sc_ref[kt]
            for h in range(H):
                lg = jnp.dot(kv, dqt_ref[h], preferred_element_type=F32) + (mask + bias_ref[h, near])
                lg_ref[h, kt] = lg
                m_acc[h] = jnp.maximum(m_acc[h], fold(lg, jnp.max))
        return tuple(m_acc)

    m_acc = lax.fori_loop(0, ngroups, logits_group, (jnp.full((SUBLANES, TQ), neg_inf, F32),) * H)
    m_row = [jnp.max(m, axis=0, keepdims=True) for m in m_acc]

    acc_ref[...] = jnp.zeros_like(acc_ref)

    def pv_group(g, l_acc):
        l_acc = list(l_acc)
        for u in range(R):
            kt = g * R + u
            kv = key_rows(ckv_ref, kt)
            for h in range(H):
                p = jnp.exp(lg_ref[h, kt] - m_row[h])
                l_acc[h] = l_acc[h] + fold(p, jnp.sum)
                acc_ref[h] += _tn_dot(kv, p.astype(BF16))
        return tuple(l_acc)

    l_acc = lax.fori_loop(0, ngroups, pv_group, (jnp.zeros((SUBLANES, TQ), F32),) * H)
    for h in range(H):
        out_t = acc_ref[h] / jnp.sum(l_acc[h], axis=0, keepdims=True)
        o_ref[0, :, h * LANES:(h + 1) * LANES] = out_t.T.astype(o_ref.dtype)


def _dsa(z, rel_bias, batch, seq):
    TQ, TK = DSA_TQ, DSA_TK
    nq, nk = seq // TQ, seq // TK
    top_k = min(TOPK_MAX, seq // 4)
    H = DSA_HEADS

    def r3(a):
        return a.reshape(batch, seq, a.shape[-1])

    misc = r3(z["misc"])
    return pl.pallas_call(
        functools.partial(_dsa_kernel, top_k=top_k),
        grid=(batch, nq),
        in_specs=[pl.BlockSpec((1, TQ, IDX_HEADS * LANES), lambda b, q: (b, q, 0)),
                  pl.BlockSpec((1, TQ, LANES), lambda b, q: (b, q, 0)),
                  pl.BlockSpec((1, seq, LANES), lambda b, q: (b, 0, 0)),
                  pl.BlockSpec((1, seq, DSA_LATENT), lambda b, q: (b, 0, 0)),
                  pl.BlockSpec((1, TQ, H * DSA_LATENT), lambda b, q: (b, q, 0)),
                  _const_spec((H, DSA_NEAR + 1, TK, TQ))],
        out_specs=pl.BlockSpec((1, TQ, H * DSA_LATENT), lambda b, q: (b, q, 0)),
        out_shape=jax.ShapeDtypeStruct((batch, seq, H * DSA_LATENT), BF16),
        scratch_shapes=[pltpu.VMEM((nk, TK, TQ), F32),
                        pltpu.VMEM((H, nk, TK, TQ), F32),
                        pltpu.VMEM((IDX_HEADS, LANES, TQ), BF16),
                        pltpu.VMEM((H, DSA_LATENT, TQ), BF16),
                        pltpu.VMEM((LANES, TQ), F32),
                        pltpu.VMEM((H, DSA_LATENT, TQ), F32)],
        compiler_params=_params(("parallel", "arbitrary")),
        name="dsa",
    )(r3(z["iq"]), misc, misc, r3(z["ckv"]), r3(z["dq"]), _dsa_bias(rel_bias))


def _merge_kernel(h_ref, og_ref, od_ref, gt_ref, wa_ref, wd_ref, wo_ref, o_ref):
    d = h_ref.shape[-1]
    ya = jnp.dot(og_ref[...], wa_ref[...], preferred_element_type=F32)
    yd = jnp.dot(od_ref[...], wd_ref[...], preferred_element_type=F32)
    gt = gt_ref[...].astype(F32)
    mix = jax.nn.sigmoid(gt[:, :d]) * ya + jax.nn.sigmoid(gt[:, d:]) * yd
    o_ref[...] = h_ref[...] + jnp.dot(mix.astype(BF16), wo_ref[...], preferred_element_type=F32)


def _merge(h, og, od, gates, wa, wd, wo, tm):
    m, d = h.shape
    row = lambda w: pl.BlockSpec((tm, w), lambda i: (i, 0))
    return pl.pallas_call(
        _merge_kernel,
        grid=(m // tm,),
        in_specs=[row(d), row(og.shape[1]), row(od.shape[1]), row(2 * d),
                  _const_spec(wa.shape), _const_spec(wd.shape), _const_spec(wo.shape)],
        out_specs=row(d),
        out_shape=jax.ShapeDtypeStruct((m, d), F32),
        compiler_params=_params(("parallel",)),
        name="merge",
    )(h, og, od, gates, wa.astype(BF16), wd.astype(BF16), wo.astype(BF16))


def _ple_final_kernel(h_ref, p_ref, g_ref, wg_ref, wp_ref, fg_ref, o_ref):
    h = h_ref[...]
    hn = _rms(h, g_ref[...]).astype(BF16)
    gate = jax.nn.sigmoid(jnp.dot(hn, wg_ref[...], preferred_element_type=F32))
    emb = jnp.dot(p_ref[...].astype(BF16), wp_ref[...], preferred_element_type=F32)
    o_ref[...] = _rms(h + gate * emb, fg_ref[...])


def _ple_final(h, p, g, wg, wp, fg, tm):
    m, d = h.shape
    pd = p.shape[1]
    row = lambda w: pl.BlockSpec((tm, w), lambda i: (i, 0))
    return pl.pallas_call(
        _ple_final_kernel,
        grid=(m // tm,),
        in_specs=[row(d), row(pd), _const_spec((1, d)), _const_spec(wg.shape), _const_spec(wp.shape),
                  _const_spec((1, d))],
        out_specs=row(d),
        out_shape=jax.ShapeDtypeStruct((m, d), F32),
        compiler_params=_params(("parallel",)),
        name="ple_final",
    )(h, p, g.reshape(1, d), wg.astype(BF16), wp.astype(BF16), fg.reshape(1, d))


def kernel(x, p, ffn1_norm, ffn1_w_in, ffn1_w_out, mix_norm, mix_w_in, gla_alpha_w, gla_alpha_b, gla_out_norm, gla_w_out, ckv_norm, dsa_w_out, rel_bias, mix_w_out, ffn2_norm, ffn2_w_in, ffn2_w_out, ple_norm, ple_w_gate, ple_w_proj, final_norm):
    batch, seq, d = x.shape
    assert p.shape[0] == 1, "single-layer trunk"
    m = batch * seq
    tm = min(TOKEN_TILE, m)
    h = x.reshape(m, d)
    h = _ffn(h, ffn1_norm[0], ffn1_w_in[0], ffn1_w_out[0], tm)
    z = _inproj(h, mix_norm[0], mix_w_in[0], ckv_norm[0], tm)
    og = _gla(z, gla_alpha_w[0], gla_alpha_b[0], gla_out_norm[0], batch, seq)
    od = _dsa(z, rel_bias, batch, seq)
    h = _merge(h, og.reshape(m, -1), od.reshape(m, -1), z["gates"], gla_w_out[0], dsa_w_out[0], mix_w_out[0], tm)
    h = _ffn(h, ffn2_norm[0], ffn2_w_in[0], ffn2_w_out[0], tm)
    h = _ple_final(h, p[0].reshape(m, -1), ple_norm[0], ple_w_gate[0], ple_w_proj[0], final_norm, tm)
    return h.reshape(batch, seq, d)
```

```python
import functools
import math

import numpy as np
import jax
import jax.numpy as jnp
from jax import lax
from jax.experimental import pallas as pl
from jax.experimental.pallas import tpu as pltpu

F32 = jnp.float32
BF16 = jnp.bfloat16
HIGHEST = lax.Precision.HIGHEST

EPS = 1e-6
GLA_HEADS = 4
GLA_DK = 128
GLA_DV = 256
GLA_LOWRANK = 16
GLA_TAU = 16.0
DSA_HEADS = 8
DSA_LATENT = 128
IDX_HEADS = 8
IDX_DIM = 64
TOPK_MAX = 256
REL_BUCKETS = 32
REL_MAX_DIST = 128

LANES = 128
SUBLANES = 8
VMEM_LIMIT = 56 * 1024 * 1024
LOG2_E = math.log2(math.e)
F32_INF_BITS = 0x7F800000
F32_MAX = float(np.finfo(np.float32).max)

TOKEN_TILE = 512
PROJ_COLS = 512
FFN_COLS = 512
GLA_CHUNK = 128
GLA_SUB = 16
DSA_TQ = 256
DSA_TK = 256

MISC_IK = 0
MISC_GA = IDX_DIM
MISC_IW = IDX_DIM + GLA_LOWRANK


def _nt_dot(a, b):
    return lax.dot_general(a, b, (((1,), (1,)), ((), ())), preferred_element_type=F32)


def _tn_dot(a, b):
    return lax.dot_general(a, b, (((0,), (0,)), ((), ())), preferred_element_type=F32)


def _rms(x, g):
    return x * lax.rsqrt(jnp.mean(x * x, axis=-1, keepdims=True) + EPS) * g


def _const_spec(shape):
    nd = len(shape)
    return pl.BlockSpec(shape, lambda *_: (0,) * nd, pipeline_mode=pl.Buffered(1))


def _params(sem):
    return pltpu.CompilerParams(dimension_semantics=sem, vmem_limit_bytes=VMEM_LIMIT)


def _ffn_kernel(h_ref, g_ref, win_ref, wout_ref, o_ref, act_ref, *, ff):
    x = h_ref[...]
    xn = _rms(x, g_ref[...]).astype(BF16)
    for c0 in range(0, ff, FFN_COLS):
        cw = min(FFN_COLS, ff - c0)
        gate = jnp.dot(xn, win_ref[:, c0:c0 + cw], preferred_element_type=F32)
        up = jnp.dot(xn, win_ref[:, ff + c0:ff + c0 + cw], preferred_element_type=F32)
        act_ref[:, c0:c0 + cw] = (gate * jax.nn.sigmoid(gate) * up).astype(BF16)
    y = jnp.dot(act_ref[...], wout_ref[...], preferred_element_type=F32)
    o_ref[...] = x + 0.5 * y


def _ffn(h, g, w_in, w_out, tm):
    m, d = h.shape
    ff = w_out.shape[0]
    assert ff % LANES == 0
    return pl.pallas_call(
        functools.partial(_ffn_kernel, ff=ff),
        grid=(m // tm,),
        in_specs=[pl.BlockSpec((tm, d), lambda i: (i, 0)),
                  _const_spec((1, d)),
                  _const_spec((d, 2 * ff)),
                  _const_spec((ff, d))],
        out_specs=pl.BlockSpec((tm, d), lambda i: (i, 0)),
        out_shape=jax.ShapeDtypeStruct((m, d), F32),
        scratch_shapes=[pltpu.VMEM((tm, ff), BF16)],
        compiler_params=_params(("parallel",)),
        name="ffn",
    )(h, g.reshape(1, d), w_in.astype(BF16), w_out.astype(BF16))


_PROJ_OUT = (("gq", GLA_HEADS * GLA_DK, F32), ("gk", GLA_HEADS * GLA_DK, F32),
             ("gv", GLA_HEADS * GLA_DV, BF16), ("gr", GLA_HEADS * GLA_DV, BF16),
             ("dq", DSA_HEADS * DSA_LATENT, BF16), ("iq", IDX_HEADS * LANES, BF16),
             ("gates", None, BF16), ("ckv", DSA_LATENT, BF16), ("misc", LANES, F32))


def _inproj_kernel(h_ref, g_ref, ckvg_ref, wa_ref, wb_ref, *out_refs, widths):
    xn = _rms(h_ref[...], g_ref[...]).astype(BF16)
    w_ref, col = wa_ref, 0
    for (name, _, dt), width, o_ref in zip(_PROJ_OUT, widths, out_refs):
        if col == wa_ref.shape[1] and w_ref is wa_ref:
            w_ref, col = wb_ref, 0
        for c0 in range(0, width, PROJ_COLS):
            cw = min(PROJ_COLS, width - c0)
            z = jnp.dot(xn, w_ref[:, col + c0:col + c0 + cw], preferred_element_type=F32)
            if name == "ckv":
                z = _rms(z, ckvg_ref[...])
            elif name == "dq":
                z = z * (DSA_LATENT ** -0.5 * LOG2_E)
            o_ref[:, c0:c0 + cw] = z.astype(dt)
        col += width


def _inproj(h, g, w_in, ckv_g, tm):
    m, d = h.shape
    sizes = (GLA_HEADS * GLA_DK, GLA_HEADS * GLA_DK, GLA_HEADS * GLA_DV, GLA_HEADS * GLA_DV, GLA_LOWRANK,
             DSA_HEADS * DSA_LATENT, DSA_LATENT, IDX_HEADS * IDX_DIM, IDX_DIM, IDX_HEADS, 2 * d)
    pts = np.cumsum(sizes)[:-1].tolist()
    gq, gk, gv, gr, ga, dq, dkv, iq, ik, iw, gates = jnp.split(w_in, pts, axis=-1)
    iq = jnp.pad(iq.reshape(d, IDX_HEADS, IDX_DIM), ((0, 0), (0, 0), (0, LANES - IDX_DIM))).reshape(d, -1)
    misc = jnp.pad(jnp.concatenate([ik, ga, iw], axis=-1),
                   ((0, 0), (0, LANES - IDX_DIM - GLA_LOWRANK - IDX_HEADS)))
    wa = w_in[:, :pts[3]].astype(BF16)
    wb = jnp.concatenate([dq, iq, gates, dkv, misc], axis=-1).astype(BF16)
    widths = tuple(wd if wd is not None else 2 * d for _, wd, _ in _PROJ_OUT)
    assert wa.shape[1] == sum(widths[:4]) and wa.shape[1] + wb.shape[1] == sum(widths)
    outs = pl.pallas_call(
        functools.partial(_inproj_kernel, widths=widths),
        grid=(m // tm,),
        in_specs=[pl.BlockSpec((tm, d), lambda i: (i, 0)),
                  _const_spec((1, d)),
                  _const_spec((1, DSA_LATENT)),
                  _const_spec(wa.shape),
                  _const_spec(wb.shape)],
        out_specs=[pl.BlockSpec((tm, wd), lambda i: (i, 0)) for wd in widths],
        out_shape=[jax.ShapeDtypeStruct((m, wd), dt) for wd, (_, _, dt) in zip(widths, _PROJ_OUT)],
        compiler_params=_params(("parallel",)),
        name="inproj",
    )(h, g.reshape(1, d), ckv_g.reshape(1, DSA_LATENT), wa, wb)
    return dict(zip([n for n, _, _ in _PROJ_OUT], outs))


def _gla_scores_safe(qs, k, b, a_ref):
    C, SUB = GLA_CHUNK, GLA_SUB
    sub_t = lax.broadcasted_iota(jnp.int32, (SUB, 1), 0)
    out_lane = lax.broadcasted_iota(jnp.int32, (SUB, C), 1)
    for i in range(C // SUB):
        r0 = i * SUB
        b_i = b[r0:r0 + SUB]
        q_i = qs[r0:r0 + SUB]
        if i == 0:
            a_row = jnp.zeros((SUB, C), F32)
        else:
            beta = b[r0 - 1:r0]
            q_t = (q_i * jnp.exp(b_i - beta)).astype(BF16)
            k_t = (k[:r0] * jnp.exp(beta - b[:r0])).astype(BF16)
            k_t = jnp.concatenate([k_t, jnp.zeros((C - r0, GLA_DK), BF16)], axis=0)
            a_row = _nt_dot(q_t, k_t)
        for s in range(SUB):
            e = jnp.exp(jnp.where(sub_t >= s, b_i - b[r0 + s:r0 + s + 1], -jnp.inf))
            col = jnp.sum(q_i * k[r0 + s:r0 + s + 1] * e, axis=-1, keepdims=True)
            a_row = jnp.where(out_lane == r0 + s, col, a_row)
        a_ref[r0:r0 + SUB, :] = a_row


def _gla_kernel(q_ref, k_ref, v_ref, gr_ref, misc_ref, aw_ref, ab_ref, ng_ref, o_ref, st_ref, a_ref):
    C = GLA_CHUNK
    DK, DV = GLA_DK, GLA_DV

    @pl.when(pl.program_id(1) == 0)
    def _():
        st_ref[...] = jnp.zeros_like(st_ref)

    x = jnp.dot(misc_ref[0], aw_ref[...], precision=HIGHEST, preferred_element_type=F32) + ab_ref[...]
    la = (jnp.minimum(x, 0.0) - jnp.log1p(jnp.exp(-jnp.abs(x)))) / GLA_TAU
    row = lax.broadcasted_iota(jnp.int32, (C, C), 0)
    col = lax.broadcasted_iota(jnp.int32, (C, C), 1)
    causal = col <= row
    ones_tri = jnp.where(causal, 1.0, 0.0).astype(BF16)
    la_hi = la.astype(BF16)
    la_lo = (la - la_hi.astype(F32)).astype(BF16)
    b_all = (jnp.dot(ones_tri, la_hi, preferred_element_type=F32)
             + jnp.dot(ones_tri, la_lo, preferred_element_type=F32))

    def head(h):
        b = b_all[:, h * DK:(h + 1) * DK]
        k = k_ref[0, :, h * DK:(h + 1) * DK]
        qs = q_ref[0, :, h * DK:(h + 1) * DK] * (DK ** -0.5)
        return b, k, qs, qs * jnp.exp(b)

    bound = jnp.zeros((1, DK), F32)
    for h in range(GLA_HEADS):
        b, k, _, q_in = head(h)
        k_max = jnp.max(jnp.abs(k), axis=0, keepdims=True) * jnp.exp(-b[C - 1:C])
        bound = bound + jnp.maximum(jnp.max(jnp.abs(q_in), axis=0, keepdims=True), 1.0) * k_max
    factored_ok = jnp.sum(bound) < 1e30

    def finish(h, b, k, q_in, scores):
        v = v_ref[0, :, h * DV:(h + 1) * DV]
        st = st_ref[h]
        o = _nt_dot(q_in.astype(BF16), st.astype(BF16))
        o = o + jnp.dot(scores.astype(BF16), v, preferred_element_type=F32)
        o = _rms(o, ng_ref[:, h * DV:(h + 1) * DV])
        gr = gr_ref[0, :, h * DV:(h + 1) * DV].astype(F32)
        o_ref[0, :, h * DV:(h + 1) * DV] = (o * (gr * jax.nn.sigmoid(gr))).astype(o_ref.dtype)
        b_last = b[C - 1:C]
        k_dec = (k * jnp.exp(b_last - b)).astype(BF16)
        st_ref[h] = st * jnp.exp(b_last) + _tn_dot(v, k_dec)

    @pl.when(factored_ok)
    def _():
        for h in range(GLA_HEADS):
            b, k, _, q_in = head(h)
            k_out = k * jnp.exp(-b)
            scores = jnp.where(causal, _nt_dot(q_in.astype(BF16), k_out.astype(BF16)), 0.0)
            finish(h, b, k, q_in, scores)

    @pl.when(jnp.logical_not(factored_ok))
    def _():
        for h in range(GLA_HEADS):
            b, k, qs, q_in = head(h)
            _gla_scores_safe(qs, k, b, a_ref)
            finish(h, b, k, q_in, a_ref[...])


def _gla(z, alpha_w, alpha_b, norm_g, batch, seq):
    C = GLA_CHUNK
    H, DK, DV = GLA_HEADS, GLA_DK, GLA_DV

    def r3(a):
        return a.reshape(batch, seq, a.shape[-1])

    aw = jnp.zeros((LANES, H * DK), F32).at[MISC_GA:MISC_GA + GLA_LOWRANK, :].set(alpha_w)
    tok = lambda w: pl.BlockSpec((1, C, w), lambda b, c: (b, c, 0))
    return pl.pallas_call(
        _gla_kernel,
        grid=(batch, seq // C),
        in_specs=[tok(H * DK), tok(H * DK), tok(H * DV), tok(H * DV), tok(LANES),
                  _const_spec((LANES, H * DK)), _const_spec((1, H * DK)), _const_spec((1, H * DV))],
        out_specs=tok(H * DV),
        out_shape=jax.ShapeDtypeStruct((batch, seq, H * DV), BF16),
        scratch_shapes=[pltpu.VMEM((H, DV, DK), F32), pltpu.VMEM((C, C), F32)],
        compiler_params=_params(("parallel", "arbitrary")),
        name="gla",
    )(r3(z["gq"]), r3(z["gk"]), r3(z["gv"]), r3(z["gr"]), r3(z["misc"]),
      aw, alpha_b.reshape(1, H * DK), norm_g.reshape(1, H * DV))


DSA_NEAR = DSA_TQ // DSA_TK + 1


def _t5_bucket(dist):
    max_exact = REL_BUCKETS // 2
    d = jnp.maximum(dist, 0)
    ratio = jnp.maximum(d, 1).astype(F32) / max_exact
    large = max_exact + (jnp.log(ratio) / math.log(REL_MAX_DIST / max_exact)
                         * (REL_BUCKETS - max_exact)).astype(jnp.int32)
    large = jnp.minimum(large, REL_BUCKETS - 1)
    return jnp.where(d < max_exact, d, large)


def _dsa_bias_kernel(rb_ref, bkt_ref, o_ref):
    h = pl.program_id(0)
    bkt = bkt_ref[...]
    far = rb_ref[REL_BUCKETS - 1, h]
    acc = jnp.zeros(bkt.shape, F32)
    for j in range(REL_BUCKETS):
        acc = jnp.where(bkt == j, rb_ref[j, h] - far, acc)
    o_ref[0] = acc * LOG2_E


def _dsa_bias(rel_bias):
    assert DSA_TK + 1 >= REL_MAX_DIST
    s = jnp.arange(DSA_TK, dtype=jnp.int32)[:, None]
    t = jnp.arange(DSA_TQ, dtype=jnp.int32)[None, :]
    near = [_t5_bucket(t - s + DSA_TK * (1 - j)) for j in range(DSA_NEAR)]
    bkt = jnp.stack([jnp.full((DSA_TK, DSA_TQ), -1, jnp.int32)] + near)
    n = DSA_NEAR + 1
    return pl.pallas_call(
        _dsa_bias_kernel,
        grid=(DSA_HEADS,),
        in_specs=[pl.BlockSpec(memory_space=pltpu.SMEM),
                  pl.BlockSpec((n, DSA_TK, DSA_TQ), lambda h: (0, 0, 0))],
        out_specs=pl.BlockSpec((1, n, DSA_TK, DSA_TQ), lambda h: (h, 0, 0, 0)),
        out_shape=jax.ShapeDtypeStruct((DSA_HEADS, n, DSA_TK, DSA_TQ), F32),
        name="dsa_bias",
    )(rel_bias, bkt)


def _dsa_kernel(iq_ref, mq_ref, mk_ref, ckv_ref, dq_ref, bias_ref, o_ref,
                sc_ref, lg_ref, iqt_ref, dqt_ref, wt_ref, acc_ref, *, top_k):
    TQ, TK = DSA_TQ, DSA_TK
    R = TQ // TK
    H = DSA_HEADS
    qi = pl.program_id(1)
    ngroups = qi + 1
    neg_inf = -jnp.inf
    kf = float(top_k)

    wt_ref[...] = mq_ref[0].T * ((IDX_HEADS ** -0.5) * (IDX_DIM ** -0.5))
    for h in range(H):
        iqt_ref[h] = iq_ref[0, :, h * LANES:(h + 1) * LANES].T
        dqt_ref[h] = dq_ref[0, :, h * LANES:(h + 1) * LANES].T

    def key_rows(ref, kt):
        return ref[0, pl.ds(pl.multiple_of(kt * TK, TK), TK), :]

    def fold(x, op):
        return op(x.reshape(TK // SUBLANES, SUBLANES, TQ), axis=0)

    k_pos = lax.broadcasted_iota(jnp.int32, (TK, TQ), 0)
    q_pos = lax.broadcasted_iota(jnp.int32, (TK, TQ), 1) + qi * TQ

    def score_group(g, carry):
        for u in range(R):
            kt = g * R + u
            keys = key_rows(mk_ref, kt).astype(BF16)
            acc = jnp.zeros((TK, TQ), F32)
            for h in range(IDX_HEADS):
                d = jnp.dot(keys, iqt_ref[h], preferred_element_type=F32)
                acc = acc + jnp.maximum(d, 0.0) * wt_ref[MISC_IW + h:MISC_IW + h + 1, :]
            sc_ref[kt] = jnp.where(k_pos + kt * TK <= q_pos, acc, neg_inf)
        return carry

    lax.fori_loop(0, ngroups, score_group, 0)

    def count(pred, cand):
        rows = 4 * SUBLANES

        def body(g, acc):
            for u in range(R):
                hit = jnp.where(pred(sc_ref[g * R + u], cand), 1.0, 0.0)
                acc = acc + jnp.sum(hit.reshape(TK // rows, rows, TQ), axis=0)
            return acc

        acc = lax.fori_loop(0, ngroups, body, jnp.zeros((rows, TQ), F32))
        return jnp.sum(acc, axis=0, keepdims=True)

    ge = lambda a, c: a >= c

    nonneg = count(ge, jnp.zeros((1, TQ), F32)) >= kf

    def cand_of(bits):
        return jnp.where(nonneg, pltpu.bitcast(bits, F32), -pltpu.bitcast(F32_INF_BITS - bits, F32))

    def bisect(_, lohi):
        lo, hi = lohi
        mid = lo + ((hi - lo) >> 1)
        ok = count(ge, cand_of(mid)) >= kf
        return jnp.where(ok, mid, lo), jnp.where(ok, hi, mid)

    lo, _ = lax.fori_loop(0, 31, bisect, (jnp.zeros((1, TQ), jnp.int32),
                                          jnp.full((1, TQ), F32_INF_BITS + 1, jnp.int32)))
    thr = jnp.maximum(cand_of(lo), -F32_MAX)
    has_ties = jnp.max(count(ge, thr)) > kf

    @pl.when(jnp.logical_not(has_ties))
    def _():
        def body(g, carry):
            for u in range(R):
                kt = g * R + u
                sc_ref[kt] = jnp.where(sc_ref[kt] >= thr, 0.0, neg_inf)
            return carry

        lax.fori_loop(0, ngroups, body, 0)

    @pl.when(has_ties)
    def _():
        need = kf - count(lambda a, c: a > c, thr)
        incl = jnp.where(lax.broadcasted_iota(jnp.int32, (TK, TK), 1)
                         <= lax.broadcasted_iota(jnp.int32, (TK, TK), 0), 1.0, 0.0).astype(BF16)

        def body(g, seen):
            for u in range(R):
                kt = g * R + u
                sc = sc_ref[kt]
                eq = jnp.where(sc == thr, 1.0, 0.0)
                rank = seen + jnp.dot(incl, eq.astype(BF16), preferred_element_type=F32)
                keep = jnp.where(sc > thr, 1.0, jnp.where(rank <= need, eq, 0.0))
                sc_ref[kt] = jnp.where(keep > 0.0, 0.0, neg_inf)
                seen = seen + jnp.sum(eq, axis=0, keepdims=True)
            return seen

        lax.fori_loop(0, ngroups, body, jnp.zeros((1, TQ), F32))

    def logits_group(g, m_acc):
        m_acc = list(m_acc)
        for u in range(R):
            kt = g * R + u
            kv = key_rows(ckv_ref, kt)
            near = jnp.clip(kt - (R * qi - 1), -1, DSA_NEAR - 1) + 1
            mask = sc_ref[kt]
            for h in range(H):
                lg = jnp.dot(kv, dqt_ref[h], preferred_element_type=F32) + (mask + bias_ref[h, near])
                lg_ref[h, kt] = lg
                m_acc[h] = jnp.maximum(m_acc[h], fold(lg, jnp.max))
        return tuple(m_acc)

    m_acc = lax.fori_loop(0, ngroups, logits_group, (jnp.full((SUBLANES, TQ), neg_inf, F32),) * H)
    m_row = [jnp.max(m, axis=0, keepdims=True) for m in m_acc]

    acc_ref[...] = jnp.zeros_like(acc_ref)

    def pv_group(g, l_acc):
        l_acc = list(l_acc)
        for u in range(R):
            kt = g * R + u
            kv = key_rows(ckv_ref, kt)
            for h in range(H):
                p = jnp.exp2(lg_ref[h, kt] - m_row[h])
                l_acc[h] = l_acc[h] + fold(p, jnp.sum)
                acc_ref[h] += _tn_dot(kv, p.astype(BF16))
        return tuple(l_acc)

    l_acc = lax.fori_loop(0, ngroups, pv_group, (jnp.zeros((SUBLANES, TQ), F32),) * H)
    for h in range(H):
        out_t = acc_ref[h] / jnp.sum(l_acc[h], axis=0, keepdims=True)
        o_ref[0, :, h * LANES:(h + 1) * LANES] = out_t.T.astype(o_ref.dtype)


def _dsa(z, rel_bias, batch, seq):
    TQ, TK = DSA_TQ, DSA_TK
    nq, nk = seq // TQ, seq // TK
    top_k = min(TOPK_MAX, seq // 4)
    H = DSA_HEADS

    def r3(a):
        return a.reshape(batch, seq, a.shape[-1])

    misc = r3(z["misc"])
    return pl.pallas_call(
        functools.partial(_dsa_kernel, top_k=top_k),
        grid=(batch, nq),
        in_specs=[pl.BlockSpec((1, TQ, IDX_HEADS * LANES), lambda b, q: (b, q, 0)),
                  pl.BlockSpec((1, TQ, LANES), lambda b, q: (b, q, 0)),
                  pl.BlockSpec((1, seq, LANES), lambda b, q: (b, 0, 0)),
                  pl.BlockSpec((1, seq, DSA_LATENT), lambda b, q: (b, 0, 0)),
                  pl.BlockSpec((1, TQ, H * DSA_LATENT), lambda b, q: (b, q, 0)),
                  _const_spec((H, DSA_NEAR + 1, TK, TQ))],
        out_specs=pl.BlockSpec((1, TQ, H * DSA_LATENT), lambda b, q: (b, q, 0)),
        out_shape=jax.ShapeDtypeStruct((batch, seq, H * DSA_LATENT), BF16),
        scratch_shapes=[pltpu.VMEM((nk, TK, TQ), F32),
                        pltpu.VMEM((H, nk, TK, TQ), F32),
                        pltpu.VMEM((IDX_HEADS, LANES, TQ), BF16),
                        pltpu.VMEM((H, DSA_LATENT, TQ), BF16),
                        pltpu.VMEM((LANES, TQ), F32),
                        pltpu.VMEM((H, DSA_LATENT, TQ), F32)],
        compiler_params=_params(("parallel", "arbitrary")),
        name="dsa",
    )(r3(z["iq"]), misc, misc, r3(z["ckv"]), r3(z["dq"]), _dsa_bias(rel_bias))


def _merge_kernel(h_ref, og_ref, od_ref, gt_ref, wa_ref, wd_ref, wo_ref, o_ref):
    d = h_ref.shape[-1]
    ya = jnp.dot(og_ref[...], wa_ref[...], preferred_element_type=F32)
    yd = jnp.dot(od_ref[...], wd_ref[...], preferred_element_type=F32)
    gt = gt_ref[...].astype(F32)
    mix = jax.nn.sigmoid(gt[:, :d]) * ya + jax.nn.sigmoid(gt[:, d:]) * yd
    o_ref[...] = h_ref[...] + jnp.dot(mix.astype(BF16), wo_ref[...], preferred_element_type=F32)


def _merge(h, og, od, gates, wa, wd, wo, tm):
    m, d = h.shape
    row = lambda w: pl.BlockSpec((tm, w), lambda i: (i, 0))
    return pl.pallas_call(
        _merge_kernel,
        grid=(m // tm,),
        in_specs=[row(d), row(og.shape[1]), row(od.shape[1]), row(2 * d),
                  _const_spec(wa.shape), _const_spec(wd.shape), _const_spec(wo.shape)],
        out_specs=row(d),
        out_shape=jax.ShapeDtypeStruct((m, d), F32),
        compiler_params=_params(("parallel",)),
        name="merge",
    )(h, og, od, gates, wa.astype(BF16), wd.astype(BF16), wo.astype(BF16))


def _ple_final_kernel(h_ref, p_ref, g_ref, wg_ref, wp_ref, fg_ref, o_ref):
    h = h_ref[...]
    hn = _rms(h, g_ref[...]).astype(BF16)
    gate = jax.nn.sigmoid(jnp.dot(hn, wg_ref[...], preferred_element_type=F32))
    emb = jnp.dot(p_ref[...].astype(BF16), wp_ref[...], preferred_element_type=F32)
    o_ref[...] = _rms(h + gate * emb, fg_ref[...])


def _ple_final(h, p, g, wg, wp, fg, tm):
    m, d = h.shape
    pd = p.shape[1]
    row = lambda w: pl.BlockSpec((tm, w), lambda i: (i, 0))
    return pl.pallas_call(
        _ple_final_kernel,
        grid=(m // tm,),
        in_specs=[row(d), row(pd), _const_spec((1, d)), _const_spec(wg.shape), _const_spec(wp.shape),
                  _const_spec((1, d))],
        out_specs=row(d),
        out_shape=jax.ShapeDtypeStruct((m, d), F32),
        compiler_params=_params(("parallel",)),
        name="ple_final",
    )(h, p, g.reshape(1, d), wg.astype(BF16), wp.astype(BF16), fg.reshape(1, d))


def kernel(x, p, ffn1_norm, ffn1_w_in, ffn1_w_out, mix_norm, mix_w_in, gla_alpha_w, gla_alpha_b, gla_out_norm, gla_w_out, ckv_norm, dsa_w_out, rel_bias, mix_w_out, ffn2_norm, ffn2_w_in, ffn2_w_out, ple_norm, ple_w_gate, ple_w_proj, final_norm):
    batch, seq, d = x.shape
    assert p.shape[0] == 1, "single-layer trunk"
    m = batch * seq
    tm = min(TOKEN_TILE, m)
    h = x.reshape(m, d)
    h = _ffn(h, ffn1_norm[0], ffn1_w_in[0], ffn1_w_out[0], tm)
    z = _inproj(h, mix_norm[0], mix_w_in[0], ckv_norm[0], tm)
    og = _gla(z, gla_alpha_w[0], gla_alpha_b[0], gla_out_norm[0], batch, seq)
    od = _dsa(z, rel_bias, batch, seq)
    h = _merge(h, og.reshape(m, -1), od.reshape(m, -1), z["gates"], gla_w_out[0], dsa_w_out[0], mix_w_out[0], tm)
    h = _ffn(h, ffn2_norm[0], ffn2_w_in[0], ffn2_w_out[0], tm)
    h = _ple_final(h, p[0].reshape(m, -1), ple_norm[0], ple_w_gate[0], ple_w_proj[0], final_norm, tm)
    return h.reshape(batch, seq, d)
```

```python
import functools
import math

import numpy as np
import jax
import jax.numpy as jnp
from jax import lax
from jax.experimental import pallas as pl
from jax.experimental.pallas import tpu as pltpu

F32 = jnp.float32
BF16 = jnp.bfloat16
HIGHEST = lax.Precision.HIGHEST

EPS = 1e-6
GLA_HEADS = 4
GLA_DK = 128
GLA_DV = 256
GLA_LOWRANK = 16
GLA_TAU = 16.0
DSA_HEADS = 8
DSA_LATENT = 128
IDX_HEADS = 8
IDX_DIM = 64
TOPK_MAX = 256
REL_BUCKETS = 32
REL_MAX_DIST = 128

LANES = 128
SUBLANES = 8
VMEM_LIMIT = 56 * 1024 * 1024
LOG2_E = math.log2(math.e)
F32_INF_BITS = 0x7F800000
F32_MAX = float(np.finfo(np.float32).max)

TOKEN_TILE = 512
PROJ_COLS = 512
FFN_COLS = 512
GLA_CHUNK = 128
GLA_SUB = 16
DSA_TQ = 256
DSA_TK = 256

MISC_IK = 0
MISC_GA = IDX_DIM
MISC_IW = IDX_DIM + GLA_LOWRANK


def _nt_dot(a, b):
    return lax.dot_general(a, b, (((1,), (1,)), ((), ())), preferred_element_type=F32)


def _tn_dot(a, b):
    return lax.dot_general(a, b, (((0,), (0,)), ((), ())), preferred_element_type=F32)


def _rms(x, g):
    return x * lax.rsqrt(jnp.mean(x * x, axis=-1, keepdims=True) + EPS) * g


def _const_spec(shape):
    nd = len(shape)
    return pl.BlockSpec(shape, lambda *_: (0,) * nd, pipeline_mode=pl.Buffered(1))


def _params(sem):
    return pltpu.CompilerParams(dimension_semantics=sem, vmem_limit_bytes=VMEM_LIMIT)


def _ffn_kernel(h_ref, g_ref, win_ref, wout_ref, o_ref, act_ref, *, ff):
    x = h_ref[...]
    xn = _rms(x, g_ref[...]).astype(BF16)
    for c0 in range(0, ff, FFN_COLS):
        cw = min(FFN_COLS, ff - c0)
        gate = jnp.dot(xn, win_ref[:, c0:c0 + cw], preferred_element_type=F32)
        up = jnp.dot(xn, win_ref[:, ff + c0:ff + c0 + cw], preferred_element_type=F32)
        act_ref[:, c0:c0 + cw] = (gate * jax.nn.sigmoid(gate) * up).astype(BF16)
    y = jnp.dot(act_ref[...], wout_ref[...], preferred_element_type=F32)
    o_ref[...] = x + 0.5 * y


def _ffn(h, g, w_in, w_out, tm):
    m, d = h.shape
    ff = w_out.shape[0]
    assert ff % LANES == 0
    return pl.pallas_call(
        functools.partial(_ffn_kernel, ff=ff),
        grid=(m // tm,),
        in_specs=[pl.BlockSpec((tm, d), lambda i: (i, 0)),
                  _const_spec((1, d)),
                  _const_spec((d, 2 * ff)),
                  _const_spec((ff, d))],
        out_specs=pl.BlockSpec((tm, d), lambda i: (i, 0)),
        out_shape=jax.ShapeDtypeStruct((m, d), F32),
        scratch_shapes=[pltpu.VMEM((tm, ff), BF16)],
        compiler_params=_params(("parallel",)),
        name="ffn",
    )(h, g.reshape(1, d), w_in.astype(BF16), w_out.astype(BF16))


_PROJ_OUT = (("gq", GLA_HEADS * GLA_DK, F32), ("gk", GLA_HEADS * GLA_DK, F32),
             ("gv", GLA_HEADS * GLA_DV, BF16), ("gr", GLA_HEADS * GLA_DV, BF16),
             ("dq", DSA_HEADS * DSA_LATENT, BF16), ("iq", IDX_HEADS * LANES, BF16),
             ("gates", None, BF16), ("ckv", DSA_LATENT, BF16), ("misc", LANES, F32))


def _inproj_kernel(h_ref, g_ref, ckvg_ref, wa_ref, wb_ref, *out_refs, widths):
    xn = _rms(h_ref[...], g_ref[...]).astype(BF16)
    w_ref, col = wa_ref, 0
    for (name, _, dt), width, o_ref in zip(_PROJ_OUT, widths, out_refs):
        if col == wa_ref.shape[1] and w_ref is wa_ref:
            w_ref, col = wb_ref, 0
        for c0 in range(0, width, PROJ_COLS):
            cw = min(PROJ_COLS, width - c0)
            z = jnp.dot(xn, w_ref[:, col + c0:col + c0 + cw], preferred_element_type=F32)
            if name == "ckv":
                z = _rms(z, ckvg_ref[...])
            elif name == "dq":
                z = z * (DSA_LATENT ** -0.5 * LOG2_E)
            o_ref[:, c0:c0 + cw] = z.astype(dt)
        col += width


def _inproj(h, g, w_in, ckv_g, tm):
    m, d = h.shape
    sizes = (GLA_HEADS * GLA_DK, GLA_HEADS * GLA_DK, GLA_HEADS * GLA_DV, GLA_HEADS * GLA_DV, GLA_LOWRANK,
             DSA_HEADS * DSA_LATENT, DSA_LATENT, IDX_HEADS * IDX_DIM, IDX_DIM, IDX_HEADS, 2 * d)
    pts = np.cumsum(sizes)[:-1].tolist()
    gq, gk, gv, gr, ga, dq, dkv, iq, ik, iw, gates = jnp.split(w_in, pts, axis=-1)
    iq = jnp.pad(iq.reshape(d, IDX_HEADS, IDX_DIM), ((0, 0), (0, 0), (0, LANES - IDX_DIM))).reshape(d, -1)
    misc = jnp.pad(jnp.concatenate([ik, ga, iw], axis=-1),
                   ((0, 0), (0, LANES - IDX_DIM - GLA_LOWRANK - IDX_HEADS)))
    wa = w_in[:, :pts[3]].astype(BF16)
    wb = jnp.concatenate([dq, iq, gates, dkv, misc], axis=-1).astype(BF16)
    widths = tuple(wd if wd is not None else 2 * d for _, wd, _ in _PROJ_OUT)
    assert wa.shape[1] == sum(widths[:4]) and wa.shape[1] + wb.shape[1] == sum(widths)
    outs = pl.pallas_call(
        functools.partial(_inproj_kernel, widths=widths),
        grid=(m // tm,),
        in_specs=[pl.BlockSpec((tm, d), lambda i: (i, 0)),
                  _const_spec((1, d)),
                  _const_spec((1, DSA_LATENT)),
                  _const_spec(wa.shape),
                  _const_spec(wb.shape)],
        out_specs=[pl.BlockSpec((tm, wd), lambda i: (i, 0)) for wd in widths],
        out_shape=[jax.ShapeDtypeStruct((m, wd), dt) for wd, (_, _, dt) in zip(widths, _PROJ_OUT)],
        compiler_params=_params(("parallel",)),
        name="inproj",
    )(h, g.reshape(1, d), ckv_g.reshape(1, DSA_LATENT), wa, wb)
    return dict(zip([n for n, _, _ in _PROJ_OUT], outs))


def _gla_scores_safe(qs, k, b, a_ref):
    C, SUB = GLA_CHUNK, GLA_SUB
    sub_t = lax.broadcasted_iota(jnp.int32, (SUB, 1), 0)
    out_lane = lax.broadcasted_iota(jnp.int32, (SUB, C), 1)
    for i in range(C // SUB):
        r0 = i * SUB
        b_i = b[r0:r0 + SUB]
        q_i = qs[r0:r0 + SUB]
        if i == 0:
            a_row = jnp.zeros((SUB, C), F32)
        else:
            beta = b[r0 - 1:r0]
            q_t = (q_i * jnp.exp(b_i - beta)).astype(BF16)
            k_t = (k[:r0] * jnp.exp(beta - b[:r0])).astype(BF16)
            k_t = jnp.concatenate([k_t, jnp.zeros((C - r0, GLA_DK), BF16)], axis=0)
            a_row = _nt_dot(q_t, k_t)
        for s in range(SUB):
            e = jnp.exp(jnp.where(sub_t >= s, b_i - b[r0 + s:r0 + s + 1], -jnp.inf))
            col = jnp.sum(q_i * k[r0 + s:r0 + s + 1] * e, axis=-1, keepdims=True)
            a_row = jnp.where(out_lane == r0 + s, col, a_row)
        a_ref[r0:r0 + SUB, :] = a_row


def _gla_kernel(q_ref, k_ref, v_ref, gr_ref, misc_ref, aw_ref, ab_ref, ng_ref, o_ref, st_ref, a_ref):
    C = GLA_CHUNK
    DK, DV = GLA_DK, GLA_DV

    @pl.when(pl.program_id(1) == 0)
    def _():
        st_ref[...] = jnp.zeros_like(st_ref)

    x = jnp.dot(misc_ref[0], aw_ref[...], precision=HIGHEST, preferred_element_type=F32) + ab_ref[...]
    la = (jnp.minimum(x, 0.0) - jnp.log1p(jnp.exp(-jnp.abs(x)))) / GLA_TAU
    row = lax.broadcasted_iota(jnp.int32, (C, C), 0)
    col = lax.broadcasted_iota(jnp.int32, (C, C), 1)
    causal = col <= row
    ones_tri = jnp.where(causal, 1.0, 0.0).astype(BF16)
    la_hi = la.astype(BF16)
    la_lo = (la - la_hi.astype(F32)).astype(BF16)
    b_all = (jnp.dot(ones_tri, la_hi, preferred_element_type=F32)
             + jnp.dot(ones_tri, la_lo, preferred_element_type=F32))

    def head(h):
        b = b_all[:, h * DK:(h + 1) * DK]
        k = k_ref[0, :, h * DK:(h + 1) * DK]
        qs = q_ref[0, :, h * DK:(h + 1) * DK] * (DK ** -0.5)
        return b, k, qs, qs * jnp.exp(b)

    bound = jnp.zeros((1, DK), F32)
    for h in range(GLA_HEADS):
        b, k, _, q_in = head(h)
        k_max = jnp.max(jnp.abs(k), axis=0, keepdims=True) * jnp.exp(-b[C - 1:C])
        bound = bound + jnp.maximum(jnp.max(jnp.abs(q_in), axis=0, keepdims=True), 1.0) * k_max
    factored_ok = jnp.sum(bound) < 1e30

    def finish(h, b, k, q_in, scores):
        v = v_ref[0, :, h * DV:(h + 1) * DV]
        st = st_ref[h]
        o = _nt_dot(q_in.astype(BF16), st.astype(BF16))
        o = o + jnp.dot(scores.astype(BF16), v, preferred_element_type=F32)
        o = _rms(o, ng_ref[:, h * DV:(h + 1) * DV])
        gr = gr_ref[0, :, h * DV:(h + 1) * DV].astype(F32)
        o_ref[0, :, h * DV:(h + 1) * DV] = (o * (gr * jax.nn.sigmoid(gr))).astype(o_ref.dtype)
        b_last = b[C - 1:C]
        k_dec = (k * jnp.exp(b_last - b)).astype(BF16)
        st_ref[h] = st * jnp.exp(b_last) + _tn_dot(v, k_dec)

    @pl.when(factored_ok)
    def _():
        for h in range(GLA_HEADS):
            b, k, _, q_in = head(h)
            k_out = k * jnp.exp(-b)
            scores = jnp.where(causal, _nt_dot(q_in.astype(BF16), k_out.astype(BF16)), 0.0)
            finish(h, b, k, q_in, scores)

    @pl.when(jnp.logical_not(factored_ok))
    def _():
        for h in range(GLA_HEADS):
            b, k, qs, q_in = head(h)
            _gla_scores_safe(qs, k, b, a_ref)
            finish(h, b, k, q_in, a_ref[...])


def _gla(z, alpha_w, alpha_b, norm_g, batch, seq):
    C = GLA_CHUNK
    H, DK, DV = GLA_HEADS, GLA_DK, GLA_DV

    def r3(a):
        return a.reshape(batch, seq, a.shape[-1])

    aw = jnp.zeros((LANES, H * DK), F32).at[MISC_GA:MISC_GA + GLA_LOWRANK, :].set(alpha_w)
    tok = lambda w: pl.BlockSpec((1, C, w), lambda b, c: (b, c, 0))
    return pl.pallas_call(
        _gla_kernel,
        grid=(batch, seq // C),
        in_specs=[tok(H * DK), tok(H * DK), tok(H * DV), tok(H * DV), tok(LANES),
                  _const_spec((LANES, H * DK)), _const_spec((1, H * DK)), _const_spec((1, H * DV))],
        out_specs=tok(H * DV),
        out_shape=jax.ShapeDtypeStruct((batch, seq, H * DV), BF16),
        scratch_shapes=[pltpu.VMEM((H, DV, DK), F32), pltpu.VMEM((C, C), F32)],
        compiler_params=_params(("parallel", "arbitrary")),
        name="gla",
    )(r3(z["gq"]), r3(z["gk"]), r3(z["gv"]), r3(z["gr"]), r3(z["misc"]),
      aw, alpha_b.reshape(1, H * DK), norm_g.reshape(1, H * DV))


DSA_NEAR = DSA_TQ // DSA_TK + 1


def _t5_bucket(dist):
    max_exact = REL_BUCKETS // 2
    d = np.maximum(dist, 0)
    steps = (np.log(np.maximum(d, 1).astype(np.float64) / max_exact) / math.log(REL_MAX_DIST / max_exact)
             * (REL_BUCKETS - max_exact))
    live = (d > max_exact) & (steps < REL_BUCKETS - max_exact - 0.5)
    assert np.all(np.abs(steps - np.round(steps))[live] > 1e-4)
    large = np.minimum(max_exact + steps.astype(np.int32), REL_BUCKETS - 1)
    return np.where(d < max_exact, d, large).astype(np.int32)


def _dsa_bias_kernel(rb_ref, bkt_ref, o_ref):
    h = pl.program_id(0)
    bkt = bkt_ref[...]
    far = rb_ref[REL_BUCKETS - 1, h]
    acc = jnp.zeros(bkt.shape, F32)
    for j in range(REL_BUCKETS):
        acc = jnp.where(bkt == j, rb_ref[j, h] - far, acc)
    o_ref[0] = acc * LOG2_E


def _dsa_bias(rel_bias):
    assert DSA_TK + 1 >= REL_MAX_DIST
    s = np.arange(DSA_TK, dtype=np.int32)[:, None]
    t = np.arange(DSA_TQ, dtype=np.int32)[None, :]
    near = [_t5_bucket(t - s + DSA_TK * (1 - j)) for j in range(DSA_NEAR)]
    bkt = jnp.asarray(np.stack([np.full((DSA_TK, DSA_TQ), -1, np.int32)] + near))
    n = DSA_NEAR + 1
    return pl.pallas_call(
        _dsa_bias_kernel,
        grid=(DSA_HEADS,),
        in_specs=[pl.BlockSpec(memory_space=pltpu.SMEM),
                  pl.BlockSpec((n, DSA_TK, DSA_TQ), lambda h: (0, 0, 0))],
        out_specs=pl.BlockSpec((1, n, DSA_TK, DSA_TQ), lambda h: (h, 0, 0, 0)),
        out_shape=jax.ShapeDtypeStruct((DSA_HEADS, n, DSA_TK, DSA_TQ), F32),
        name="dsa_bias",
    )(rel_bias, bkt)


def _dsa_kernel(iq_ref, mq_ref, mk_ref, ckv_ref, dq_ref, bias_ref, o_ref,
                sc_ref, lg_ref, iqt_ref, dqt_ref, wt_ref, acc_ref, *, top_k):
    TQ, TK = DSA_TQ, DSA_TK
    R = TQ // TK
    H = DSA_HEADS
    qi = pl.program_id(1)
    ngroups = qi + 1
    neg_inf = -jnp.inf
    kf = float(top_k)

    wt_ref[...] = mq_ref[0].T * ((IDX_HEADS ** -0.5) * (IDX_DIM ** -0.5))
    for h in range(H):
        iqt_ref[h] = iq_ref[0, :, h * LANES:(h + 1) * LANES].T
        dqt_ref[h] = dq_ref[0, :, h * LANES:(h + 1) * LANES].T

    def key_rows(ref, kt):
        return ref[0, pl.ds(pl.multiple_of(kt * TK, TK), TK), :]

    def fold(x, op):
        return op(x.reshape(TK // SUBLANES, SUBLANES, TQ), axis=0)

    k_pos = lax.broadcasted_iota(jnp.int32, (TK, TQ), 0)
    q_pos = lax.broadcasted_iota(jnp.int32, (TK, TQ), 1) + qi * TQ

    def score_group(g, carry):
        for u in range(R):
            kt = g * R + u
            keys = key_rows(mk_ref, kt).astype(BF16)
            acc = jnp.zeros((TK, TQ), F32)
            for h in range(IDX_HEADS):
                d = jnp.dot(keys, iqt_ref[h], preferred_element_type=F32)
                acc = acc + jnp.maximum(d, 0.0) * wt_ref[MISC_IW + h:MISC_IW + h + 1, :]
            sc_ref[kt] = jnp.where(k_pos + kt * TK <= q_pos, acc, neg_inf)
        return carry

    lax.fori_loop(0, ngroups, score_group, 0)

    def count_ge(cand):
        rows = 4 * SUBLANES

        def body(g, acc):
            for u in range(R):
                hit = jnp.where(sc_ref[g * R + u] >= cand, 1.0, 0.0)
                acc = acc + jnp.sum(hit.reshape(TK // rows, rows, TQ), axis=0)
            return acc

        acc = lax.fori_loop(0, ngroups, body, jnp.zeros((rows, TQ), F32))
        return jnp.sum(acc, axis=0, keepdims=True)

    n_zero = count_ge(jnp.zeros((1, TQ), F32))
    nonneg = n_zero >= kf

    def cand_of(bits):
        return jnp.where(nonneg, pltpu.bitcast(bits, F32), -pltpu.bitcast(F32_INF_BITS - bits, F32))

    def bisect(_, state):
        lo, hi, n_lo = state
        mid = lo + ((hi - lo) >> 1)
        n_mid = count_ge(cand_of(mid))
        ok = n_mid >= kf
        return jnp.where(ok, mid, lo), jnp.where(ok, hi, mid), jnp.where(ok, n_mid, n_lo)

    lo, _, n_ge = lax.fori_loop(0, 31, bisect, (jnp.zeros((1, TQ), jnp.int32),
                                               jnp.full((1, TQ), F32_INF_BITS + 1, jnp.int32),
                                               jnp.where(nonneg, n_zero, kf)))
    thr = jnp.maximum(cand_of(lo), -F32_MAX)
    has_ties = jnp.max(n_ge) > kf

    @pl.when(jnp.logical_not(has_ties))
    def _():
        def body(g, carry):
            for u in range(R):
                kt = g * R + u
                sc_ref[kt] = jnp.where(sc_ref[kt] >= thr, 0.0, neg_inf)
            return carry

        lax.fori_loop(0, ngroups, body, 0)

    @pl.when(has_ties)
    def _():
        def count_gt(g, acc):
            for u in range(R):
                acc = acc + fold(jnp.where(sc_ref[g * R + u] > thr, 1.0, 0.0), jnp.sum)
            return acc

        n_gt = lax.fori_loop(0, ngroups, count_gt, jnp.zeros((SUBLANES, TQ), F32))
        need = kf - jnp.sum(n_gt, axis=0, keepdims=True)
        incl = jnp.where(lax.broadcasted_iota(jnp.int32, (TK, TK), 1)
                         <= lax.broadcasted_iota(jnp.int32, (TK, TK), 0), 1.0, 0.0).astype(BF16)

        def body(g, seen):
            for u in range(R):
                kt = g * R + u
                sc = sc_ref[kt]
                eq = jnp.where(sc == thr, 1.0, 0.0)
                rank = seen + jnp.dot(incl, eq.astype(BF16), preferred_element_type=F32)
                keep = jnp.where(sc > thr, 1.0, jnp.where(rank <= need, eq, 0.0))
                sc_ref[kt] = jnp.where(keep > 0.0, 0.0, neg_inf)
                seen = seen + jnp.sum(eq, axis=0, keepdims=True)
            return seen

        lax.fori_loop(0, ngroups, body, jnp.zeros((1, TQ), F32))

    def logits_group(g, m_acc):
        m_acc = list(m_acc)
        for u in range(R):
            kt = g * R + u
            kv = key_rows(ckv_ref, kt)
            near = jnp.clip(kt - (R * qi - 1), -1, DSA_NEAR - 1) + 1
            mask = sc_ref[kt]
            for h in range(H):
                lg = jnp.dot(kv, dqt_ref[h], preferred_element_type=F32) + (mask + bias_ref[h, near])
                lg_ref[h, kt] = lg
                m_acc[h] = jnp.maximum(m_acc[h], fold(lg, jnp.max))
        return tuple(m_acc)

    m_acc = lax.fori_loop(0, ngroups, logits_group, (jnp.full((SUBLANES, TQ), neg_inf, F32),) * H)
    m_row = [jnp.max(m, axis=0, keepdims=True) for m in m_acc]

    acc_ref[...] = jnp.zeros_like(acc_ref)

    def pv_group(g, l_acc):
        l_acc = list(l_acc)
        for u in range(R):
            kt = g * R + u
            kv = key_rows(ckv_ref, kt)
            for h in range(H):
                p = jnp.exp2(lg_ref[h, kt] - m_row[h])
                l_acc[h] = l_acc[h] + fold(p, jnp.sum)
                acc_ref[h] += _tn_dot(kv, p.astype(BF16))
        return tuple(l_acc)

    l_acc = lax.fori_loop(0, ngroups, pv_group, (jnp.zeros((SUBLANES, TQ), F32),) * H)
    for h in range(H):
        out_t = acc_ref[h] / jnp.sum(l_acc[h], axis=0, keepdims=True)
        o_ref[0, :, h * LANES:(h + 1) * LANES] = out_t.T.astype(o_ref.dtype)


def _dsa(z, rel_bias, batch, seq):
    TQ, TK = DSA_TQ, DSA_TK
    nq, nk = seq // TQ, seq // TK
    top_k = min(TOPK_MAX, seq // 4)
    H = DSA_HEADS

    def r3(a):
        return a.reshape(batch, seq, a.shape[-1])

    misc = r3(z["misc"])
    return pl.pallas_call(
        functools.partial(_dsa_kernel, top_k=top_k),
        grid=(batch, nq),
        in_specs=[pl.BlockSpec((1, TQ, IDX_HEADS * LANES), lambda b, q: (b, q, 0)),
                  pl.BlockSpec((1, TQ, LANES), lambda b, q: (b, q, 0)),
                  pl.BlockSpec((1, seq, LANES), lambda b, q: (b, 0, 0)),
                  pl.BlockSpec((1, seq, DSA_LATENT), lambda b, q: (b, 0, 0)),
                  pl.BlockSpec((1, TQ, H * DSA_LATENT), lambda b, q: (b, q, 0)),
                  _const_spec((H, DSA_NEAR + 1, TK, TQ))],
        out_specs=pl.BlockSpec((1, TQ, H * DSA_LATENT), lambda b, q: (b, q, 0)),
        out_shape=jax.ShapeDtypeStruct((batch, seq, H * DSA_LATENT), BF16),
        scratch_shapes=[pltpu.VMEM((nk, TK, TQ), F32),
                        pltpu.VMEM((H, nk, TK, TQ), F32),
                        pltpu.VMEM((IDX_HEADS, LANES, TQ), BF16),
                        pltpu.VMEM((H, DSA_LATENT, TQ), BF16),
                        pltpu.VMEM((LANES, TQ), F32),
                        pltpu.VMEM((H, DSA_LATENT, TQ), F32)],
        compiler_params=_params(("parallel", "arbitrary")),
        name="dsa",
    )(r3(z["iq"]), misc, misc, r3(z["ckv"]), r3(z["dq"]), _dsa_bias(rel_bias))


def _merge_kernel(h_ref, og_ref, od_ref, gt_ref, wa_ref, wd_ref, wo_ref, o_ref):
    d = h_ref.shape[-1]
    ya = jnp.dot(og_ref[...], wa_ref[...], preferred_element_type=F32)
    yd = jnp.dot(od_ref[...], wd_ref[...], preferred_element_type=F32)
    gt = gt_ref[...].astype(F32)
    mix = jax.nn.sigmoid(gt[:, :d]) * ya + jax.nn.sigmoid(gt[:, d:]) * yd
    o_ref[...] = h_ref[...] + jnp.dot(mix.astype(BF16), wo_ref[...], preferred_element_type=F32)


def _merge(h, og, od, gates, wa, wd, wo, tm):
    m, d = h.shape
    row = lambda w: pl.BlockSpec((tm, w), lambda i: (i, 0))
    return pl.pallas_call(
        _merge_kernel,
        grid=(m // tm,),
        in_specs=[row(d), row(og.shape[1]), row(od.shape[1]), row(2 * d),
                  _const_spec(wa.shape), _const_spec(wd.shape), _const_spec(wo.shape)],
        out_specs=row(d),
        out_shape=jax.ShapeDtypeStruct((m, d), F32),
        compiler_params=_params(("parallel",)),
        name="merge",
    )(h, og, od, gates, wa.astype(BF16), wd.astype(BF16), wo.astype(BF16))


def _ple_final_kernel(h_ref, p_ref, g_ref, wg_ref, wp_ref, fg_ref, o_ref):
    h = h_ref[...]
    hn = _rms(h, g_ref[...]).astype(BF16)
    gate = jax.nn.sigmoid(jnp.dot(hn, wg_ref[...], preferred_element_type=F32))
    emb = jnp.dot(p_ref[...].astype(BF16), wp_ref[...], preferred_element_type=F32)
    o_ref[...] = _rms(h + gate * emb, fg_ref[...])


def _ple_final(h, p, g, wg, wp, fg, tm):
    m, d = h.shape
    pd = p.shape[1]
    row = lambda w: pl.BlockSpec((tm, w), lambda i: (i, 0))
    return pl.pallas_call(
        _ple_final_kernel,
        grid=(m // tm,),
        in_specs=[row(d), row(pd), _const_spec((1, d)), _const_spec(wg.shape), _const_spec(wp.shape),
                  _const_spec((1, d))],
        out_specs=row(d),
        out_shape=jax.ShapeDtypeStruct((m, d), F32),
        compiler_params=_params(("parallel",)),
        name="ple_final",
    )(h, p, g.reshape(1, d), wg.astype(BF16), wp.astype(BF16), fg.reshape(1, d))


def kernel(x, p, ffn1_norm, ffn1_w_in, ffn1_w_out, mix_norm, mix_w_in, gla_alpha_w, gla_alpha_b, gla_out_norm, gla_w_out, ckv_norm, dsa_w_out, rel_bias, mix_w_out, ffn2_norm, ffn2_w_in, ffn2_w_out, ple_norm, ple_w_gate, ple_w_proj, final_norm):
    batch, seq, d = x.shape
    assert p.shape[0] == 1, "single-layer trunk"
    m = batch * seq
    tm = min(TOKEN_TILE, m)
    h = x.reshape(m, d)
    h = _ffn(h, ffn1_norm[0], ffn1_w_in[0], ffn1_w_out[0], tm)
    z = _inproj(h, mix_norm[0], mix_w_in[0], ckv_norm[0], tm)
    og = _gla(z, gla_alpha_w[0], gla_alpha_b[0], gla_out_norm[0], batch, seq)
    od = _dsa(z, rel_bias, batch, seq)
    h = _merge(h, og.reshape(m, -1), od.reshape(m, -1), z["gates"], gla_w_out[0], dsa_w_out[0], mix_w_out[0], tm)
    h = _ffn(h, ffn2_norm[0], ffn2_w_in[0], ffn2_w_out[0], tm)
    h = _ple_final(h, p[0].reshape(m, -1), ple_norm[0], ple_w_gate[0], ple_w_proj[0], final_norm, tm)
    return h.reshape(batch, seq, d)
```

```python
import functools
import math

import numpy as np
import jax
import jax.numpy as jnp
from jax import lax
from jax.experimental import pallas as pl
from jax.experimental.pallas import tpu as pltpu

F32 = jnp.float32
BF16 = jnp.bfloat16
HIGHEST = lax.Precision.HIGHEST

EPS = 1e-6
GLA_HEADS = 4
GLA_DK = 128
GLA_DV = 256
GLA_LOWRANK = 16
GLA_TAU = 16.0
DSA_HEADS = 8
DSA_LATENT = 128
IDX_HEADS = 8
IDX_DIM = 64
TOPK_MAX = 256
REL_BUCKETS = 32
REL_MAX_DIST = 128

LANES = 128
SUBLANES = 8
VMEM_LIMIT = 56 * 1024 * 1024
LOG2_E = math.log2(math.e)
F32_INF_BITS = 0x7F800000
F32_MAX = float(np.finfo(np.float32).max)

TOKEN_TILE = 512
PROJ_COLS = 512
FFN_COLS = 512
GLA_CHUNK = 128
GLA_SUB = 16
DSA_TQ = 256
DSA_TK = 256

MISC_IK = 0
MISC_GA = IDX_DIM
MISC_IW = IDX_DIM + GLA_LOWRANK


def _nt_dot(a, b):
    return lax.dot_general(a, b, (((1,), (1,)), ((), ())), preferred_element_type=F32)


def _tn_dot(a, b):
    return lax.dot_general(a, b, (((0,), (0,)), ((), ())), preferred_element_type=F32)


def _rms(x, g):
    return x * lax.rsqrt(jnp.mean(x * x, axis=-1, keepdims=True) + EPS) * g


def _const_spec(shape):
    nd = len(shape)
    return pl.BlockSpec(shape, lambda *_: (0,) * nd, pipeline_mode=pl.Buffered(1))


def _params(sem):
    return pltpu.CompilerParams(dimension_semantics=sem, vmem_limit_bytes=VMEM_LIMIT)


def _ffn_kernel(h_ref, g_ref, win_ref, wout_ref, o_ref, act_ref, *, ff):
    x = h_ref[...]
    xn = _rms(x, g_ref[...]).astype(BF16)
    for c0 in range(0, ff, FFN_COLS):
        cw = min(FFN_COLS, ff - c0)
        gate = jnp.dot(xn, win_ref[:, c0:c0 + cw], preferred_element_type=F32)
        up = jnp.dot(xn, win_ref[:, ff + c0:ff + c0 + cw], preferred_element_type=F32)
        act_ref[:, c0:c0 + cw] = (gate * jax.nn.sigmoid(gate) * up).astype(BF16)
    y = jnp.dot(act_ref[...], wout_ref[...], preferred_element_type=F32)
    o_ref[...] = x + 0.5 * y


def _ffn(h, g, w_in, w_out, tm):
    m, d = h.shape
    ff = w_out.shape[0]
    assert ff % LANES == 0
    return pl.pallas_call(
        functools.partial(_ffn_kernel, ff=ff),
        grid=(m // tm,),
        in_specs=[pl.BlockSpec((tm, d), lambda i: (i, 0)),
                  _const_spec((1, d)),
                  _const_spec((d, 2 * ff)),
                  _const_spec((ff, d))],
        out_specs=pl.BlockSpec((tm, d), lambda i: (i, 0)),
        out_shape=jax.ShapeDtypeStruct((m, d), F32),
        scratch_shapes=[pltpu.VMEM((tm, ff), BF16)],
        compiler_params=_params(("parallel",)),
        name="ffn",
    )(h, g.reshape(1, d), w_in.astype(BF16), w_out.astype(BF16))


_PROJ_OUT = (("gq", GLA_HEADS * GLA_DK, F32), ("gk", GLA_HEADS * GLA_DK, F32),
             ("gv", GLA_HEADS * GLA_DV, BF16), ("gr", GLA_HEADS * GLA_DV, BF16),
             ("dq", DSA_HEADS * DSA_LATENT, BF16), ("iq", IDX_HEADS * LANES, BF16),
             ("gates", None, BF16), ("ckv", DSA_LATENT, BF16), ("misc", LANES, F32))


def _inproj_kernel(h_ref, g_ref, ckvg_ref, wa_ref, wb_ref, *out_refs, widths):
    xn = _rms(h_ref[...], g_ref[...]).astype(BF16)
    w_ref, col = wa_ref, 0
    for (name, _, dt), width, o_ref in zip(_PROJ_OUT, widths, out_refs):
        if col == wa_ref.shape[1] and w_ref is wa_ref:
            w_ref, col = wb_ref, 0
        for c0 in range(0, width, PROJ_COLS):
            cw = min(PROJ_COLS, width - c0)
            z = jnp.dot(xn, w_ref[:, col + c0:col + c0 + cw], preferred_element_type=F32)
            if name == "ckv":
                z = _rms(z, ckvg_ref[...])
            elif name == "dq":
                z = z * (DSA_LATENT ** -0.5 * LOG2_E)
            o_ref[:, c0:c0 + cw] = z.astype(dt)
        col += width


def _inproj(h, g, w_in, ckv_g, tm):
    m, d = h.shape
    sizes = (GLA_HEADS * GLA_DK, GLA_HEADS * GLA_DK, GLA_HEADS * GLA_DV, GLA_HEADS * GLA_DV, GLA_LOWRANK,
             DSA_HEADS * DSA_LATENT, DSA_LATENT, IDX_HEADS * IDX_DIM, IDX_DIM, IDX_HEADS, 2 * d)
    pts = np.cumsum(sizes)[:-1].tolist()
    gq, gk, gv, gr, ga, dq, dkv, iq, ik, iw, gates = jnp.split(w_in, pts, axis=-1)
    iq = jnp.pad(iq.reshape(d, IDX_HEADS, IDX_DIM), ((0, 0), (0, 0), (0, LANES - IDX_DIM))).reshape(d, -1)
    misc = jnp.pad(jnp.concatenate([ik, ga, iw], axis=-1),
                   ((0, 0), (0, LANES - IDX_DIM - GLA_LOWRANK - IDX_HEADS)))
    wa = w_in[:, :pts[3]].astype(BF16)
    wb = jnp.concatenate([dq, iq, gates, dkv, misc], axis=-1).astype(BF16)
    widths = tuple(wd if wd is not None else 2 * d for _, wd, _ in _PROJ_OUT)
    assert wa.shape[1] == sum(widths[:4]) and wa.shape[1] + wb.shape[1] == sum(widths)
    outs = pl.pallas_call(
        functools.partial(_inproj_kernel, widths=widths),
        grid=(m // tm,),
        in_specs=[pl.BlockSpec((tm, d), lambda i: (i, 0)),
                  _const_spec((1, d)),
                  _const_spec((1, DSA_LATENT)),
                  _const_spec(wa.shape),
                  _const_spec(wb.shape)],
        out_specs=[pl.BlockSpec((tm, wd), lambda i: (i, 0)) for wd in widths],
        out_shape=[jax.ShapeDtypeStruct((m, wd), dt) for wd, (_, _, dt) in zip(widths, _PROJ_OUT)],
        compiler_params=_params(("parallel",)),
        name="inproj",
    )(h, g.reshape(1, d), ckv_g.reshape(1, DSA_LATENT), wa, wb)
    return dict(zip([n for n, _, _ in _PROJ_OUT], outs))


def _gla_scores_safe(qs, k, b, a_ref):
    C, SUB = GLA_CHUNK, GLA_SUB
    sub_t = lax.broadcasted_iota(jnp.int32, (SUB, 1), 0)
    out_lane = lax.broadcasted_iota(jnp.int32, (SUB, C), 1)
    for i in range(C // SUB):
        r0 = i * SUB
        b_i = b[r0:r0 + SUB]
        q_i = qs[r0:r0 + SUB]
        if i == 0:
            a_row = jnp.zeros((SUB, C), F32)
        else:
            beta = b[r0 - 1:r0]
            q_t = (q_i * jnp.exp(b_i - beta)).astype(BF16)
            k_t = (k[:r0] * jnp.exp(beta - b[:r0])).astype(BF16)
            k_t = jnp.concatenate([k_t, jnp.zeros((C - r0, GLA_DK), BF16)], axis=0)
            a_row = _nt_dot(q_t, k_t)
        for s in range(SUB):
            e = jnp.exp(jnp.where(sub_t >= s, b_i - b[r0 + s:r0 + s + 1], -jnp.inf))
            col = jnp.sum(q_i * k[r0 + s:r0 + s + 1] * e, axis=-1, keepdims=True)
            a_row = jnp.where(out_lane == r0 + s, col, a_row)
        a_ref[r0:r0 + SUB, :] = a_row


def _gla_kernel(q_ref, k_ref, v_ref, gr_ref, misc_ref, aw_ref, ab_ref, ng_ref, o_ref, st_ref, a_ref):
    C = GLA_CHUNK
    DK, DV = GLA_DK, GLA_DV

    @pl.when(pl.program_id(1) == 0)
    def _():
        st_ref[...] = jnp.zeros_like(st_ref)

    x = jnp.dot(misc_ref[0], aw_ref[...], precision=HIGHEST, preferred_element_type=F32) + ab_ref[...]
    la = (jnp.minimum(x, 0.0) - jnp.log1p(jnp.exp(-jnp.abs(x)))) / GLA_TAU
    row = lax.broadcasted_iota(jnp.int32, (C, C), 0)
    col = lax.broadcasted_iota(jnp.int32, (C, C), 1)
    causal = col <= row
    ones_tri = jnp.where(causal, 1.0, 0.0).astype(BF16)
    la_hi = la.astype(BF16)
    la_lo = (la - la_hi.astype(F32)).astype(BF16)
    b_all = (jnp.dot(ones_tri, la_hi, preferred_element_type=F32)
             + jnp.dot(ones_tri, la_lo, preferred_element_type=F32))

    def head(h):
        b = b_all[:, h * DK:(h + 1) * DK]
        k = k_ref[0, :, h * DK:(h + 1) * DK]
        qs = q_ref[0, :, h * DK:(h + 1) * DK] * (DK ** -0.5)
        return b, k, qs, qs * jnp.exp(b)

    bound = jnp.zeros((1, DK), F32)
    for h in range(GLA_HEADS):
        b, k, _, q_in = head(h)
        k_max = jnp.max(jnp.abs(k), axis=0, keepdims=True) * jnp.exp(-b[C - 1:C])
        bound = bound + jnp.maximum(jnp.max(jnp.abs(q_in), axis=0, keepdims=True), 1.0) * k_max
    factored_ok = jnp.sum(bound) < 1e30

    def finish(h, b, k, q_in, scores):
        v = v_ref[0, :, h * DV:(h + 1) * DV]
        st = st_ref[h]
        o = _nt_dot(q_in.astype(BF16), st.astype(BF16))
        o = o + jnp.dot(scores.astype(BF16), v, preferred_element_type=F32)
        o = _rms(o, ng_ref[:, h * DV:(h + 1) * DV])
        gr = gr_ref[0, :, h * DV:(h + 1) * DV].astype(F32)
        o_ref[0, :, h * DV:(h + 1) * DV] = (o * (gr * jax.nn.sigmoid(gr))).astype(o_ref.dtype)
        b_last = b[C - 1:C]
        k_dec = (k * jnp.exp(b_last - b)).astype(BF16)
        st_ref[h] = st * jnp.exp(b_last) + _tn_dot(v, k_dec)

    @pl.when(factored_ok)
    def _():
        for h in range(GLA_HEADS):
            b, k, _, q_in = head(h)
            k_out = k * jnp.exp(-b)
            scores = jnp.where(causal, _nt_dot(q_in.astype(BF16), k_out.astype(BF16)), 0.0)
            finish(h, b, k, q_in, scores)

    @pl.when(jnp.logical_not(factored_ok))
    def _():
        for h in range(GLA_HEADS):
            b, k, qs, q_in = head(h)
            _gla_scores_safe(qs, k, b, a_ref)
            finish(h, b, k, q_in, a_ref[...])


def _gla(z, alpha_w, alpha_b, norm_g, batch, seq):
    C = GLA_CHUNK
    H, DK, DV = GLA_HEADS, GLA_DK, GLA_DV

    def r3(a):
        return a.reshape(batch, seq, a.shape[-1])

    aw = jnp.zeros((LANES, H * DK), F32).at[MISC_GA:MISC_GA + GLA_LOWRANK, :].set(alpha_w)
    tok = lambda w: pl.BlockSpec((1, C, w), lambda b, c: (b, c, 0))
    return pl.pallas_call(
        _gla_kernel,
        grid=(batch, seq // C),
        in_specs=[tok(H * DK), tok(H * DK), tok(H * DV), tok(H * DV), tok(LANES),
                  _const_spec((LANES, H * DK)), _const_spec((1, H * DK)), _const_spec((1, H * DV))],
        out_specs=tok(H * DV),
        out_shape=jax.ShapeDtypeStruct((batch, seq, H * DV), BF16),
        scratch_shapes=[pltpu.VMEM((H, DV, DK), F32), pltpu.VMEM((C, C), F32)],
        compiler_params=_params(("parallel", "arbitrary")),
        name="gla",
    )(r3(z["gq"]), r3(z["gk"]), r3(z["gv"]), r3(z["gr"]), r3(z["misc"]),
      aw, alpha_b.reshape(1, H * DK), norm_g.reshape(1, H * DV))


DSA_NEAR = DSA_TQ // DSA_TK + 1


def _t5_bucket(dist):
    max_exact = REL_BUCKETS // 2
    d = np.maximum(dist, 0)
    steps = (np.log(np.maximum(d, 1).astype(np.float64) / max_exact) / math.log(REL_MAX_DIST / max_exact)
             * (REL_BUCKETS - max_exact))
    live = (d > max_exact) & (steps < REL_BUCKETS - max_exact - 0.5)
    assert np.all(np.abs(steps - np.round(steps))[live] > 1e-4)
    large = np.minimum(max_exact + steps.astype(np.int32), REL_BUCKETS - 1)
    return np.where(d < max_exact, d, large).astype(np.int32)


def _dsa_bias_kernel(rb_ref, bkt_ref, o_ref):
    h = pl.program_id(0)
    bkt = bkt_ref[...]
    far = rb_ref[REL_BUCKETS - 1, h]
    acc = jnp.zeros(bkt.shape, F32)
    for j in range(REL_BUCKETS):
        acc = jnp.where(bkt == j, rb_ref[j, h] - far, acc)
    o_ref[0] = acc * LOG2_E


def _dsa_bias(rel_bias):
    assert DSA_TK + 1 >= REL_MAX_DIST
    s = np.arange(DSA_TK, dtype=np.int32)[:, None]
    t = np.arange(DSA_TQ, dtype=np.int32)[None, :]
    near = [_t5_bucket(t - s + DSA_TK * (1 - j)) for j in range(DSA_NEAR)]
    bkt = jnp.asarray(np.stack([np.full((DSA_TK, DSA_TQ), -1, np.int32)] + near))
    n = DSA_NEAR + 1
    return pl.pallas_call(
        _dsa_bias_kernel,
        grid=(DSA_HEADS,),
        in_specs=[pl.BlockSpec(memory_space=pltpu.SMEM),
                  pl.BlockSpec((n, DSA_TK, DSA_TQ), lambda h: (0, 0, 0))],
        out_specs=pl.BlockSpec((1, n, DSA_TK, DSA_TQ), lambda h: (h, 0, 0, 0)),
        out_shape=jax.ShapeDtypeStruct((DSA_HEADS, n, DSA_TK, DSA_TQ), F32),
        name="dsa_bias",
    )(rel_bias, bkt)


def _dsa_kernel(iq_ref, mq_ref, mk_ref, ckv_ref, dq_ref, bias_ref, o_ref,
                sc_ref, lg_ref, iqt_ref, dqt_ref, wt_ref, acc_ref, ml_ref, *, top_k):
    TQ, TK = DSA_TQ, DSA_TK
    R = TQ // TK
    H = DSA_HEADS
    qi = pl.program_id(1)
    ngroups = qi + 1
    neg_inf = -jnp.inf
    kf = float(top_k)

    wt_ref[...] = mq_ref[0].T * ((IDX_HEADS ** -0.5) * (IDX_DIM ** -0.5))
    for h in range(H):
        iqt_ref[h] = iq_ref[0, :, h * LANES:(h + 1) * LANES].T
        dqt_ref[h] = dq_ref[0, :, h * LANES:(h + 1) * LANES].T

    def key_rows(ref, kt):
        return ref[0, pl.ds(pl.multiple_of(kt * TK, TK), TK), :]

    def fold(x, op):
        return op(x.reshape(TK // SUBLANES, SUBLANES, TQ), axis=0)

    k_pos = lax.broadcasted_iota(jnp.int32, (TK, TQ), 0)
    q_pos = lax.broadcasted_iota(jnp.int32, (TK, TQ), 1) + qi * TQ

    def for_tiles(tile_fn):
        def pair(g, carry):
            tile_fn(2 * g)
            tile_fn(2 * g + 1)
            return carry

        lax.fori_loop(0, ngroups // 2, pair, 0)

        @pl.when(ngroups % 2 == 1)
        def _():
            tile_fn(ngroups - 1)

    def score_tile(kt):
        keys = key_rows(mk_ref, kt).astype(BF16)
        acc = jnp.zeros((TK, TQ), F32)
        for h in range(IDX_HEADS):
            d = jnp.dot(keys, iqt_ref[h], preferred_element_type=F32)
            acc = acc + jnp.maximum(d, 0.0) * wt_ref[MISC_IW + h:MISC_IW + h + 1, :]
        sc_ref[kt] = jnp.where(k_pos + kt * TK <= q_pos, acc, neg_inf)

    for_tiles(score_tile)

    def count_ge(cand):
        rows = 4 * SUBLANES

        def body(g, acc):
            for u in range(R):
                hit = jnp.where(sc_ref[g * R + u] >= cand, 1.0, 0.0)
                acc = acc + jnp.sum(hit.reshape(TK // rows, rows, TQ), axis=0)
            return acc

        acc = lax.fori_loop(0, ngroups, body, jnp.zeros((rows, TQ), F32))
        return jnp.sum(acc, axis=0, keepdims=True)

    n_zero = count_ge(jnp.zeros((1, TQ), F32))
    nonneg = n_zero >= kf

    def cand_of(bits):
        return jnp.where(nonneg, pltpu.bitcast(bits, F32), -pltpu.bitcast(F32_INF_BITS - bits, F32))

    def bisect(_, state):
        lo, hi, n_lo = state
        mid = lo + ((hi - lo) >> 1)
        n_mid = count_ge(cand_of(mid))
        ok = n_mid >= kf
        return jnp.where(ok, mid, lo), jnp.where(ok, hi, mid), jnp.where(ok, n_mid, n_lo)

    lo, _, n_ge = lax.fori_loop(0, 31, bisect, (jnp.zeros((1, TQ), jnp.int32),
                                               jnp.full((1, TQ), F32_INF_BITS + 1, jnp.int32),
                                               jnp.where(nonneg, n_zero, kf)))
    thr = jnp.maximum(cand_of(lo), -F32_MAX)
    has_ties = jnp.max(n_ge) > kf

    @pl.when(jnp.logical_not(has_ties))
    def _():
        def body(g, carry):
            for u in range(R):
                kt = g * R + u
                sc_ref[kt] = jnp.where(sc_ref[kt] >= thr, 0.0, neg_inf)
            return carry

        lax.fori_loop(0, ngroups, body, 0)

    @pl.when(has_ties)
    def _():
        def count_gt(g, acc):
            for u in range(R):
                acc = acc + fold(jnp.where(sc_ref[g * R + u] > thr, 1.0, 0.0), jnp.sum)
            return acc

        n_gt = lax.fori_loop(0, ngroups, count_gt, jnp.zeros((SUBLANES, TQ), F32))
        need = kf - jnp.sum(n_gt, axis=0, keepdims=True)
        incl = jnp.where(lax.broadcasted_iota(jnp.int32, (TK, TK), 1)
                         <= lax.broadcasted_iota(jnp.int32, (TK, TK), 0), 1.0, 0.0).astype(BF16)

        def body(g, seen):
            for u in range(R):
                kt = g * R + u
                sc = sc_ref[kt]
                eq = jnp.where(sc == thr, 1.0, 0.0)
                rank = seen + jnp.dot(incl, eq.astype(BF16), preferred_element_type=F32)
                keep = jnp.where(sc > thr, 1.0, jnp.where(rank <= need, eq, 0.0))
                sc_ref[kt] = jnp.where(keep > 0.0, 0.0, neg_inf)
                seen = seen + jnp.sum(eq, axis=0, keepdims=True)
            return seen

        lax.fori_loop(0, ngroups, body, jnp.zeros((1, TQ), F32))

    m_ref, l_ref = ml_ref.at[0], ml_ref.at[1]
    m_ref[...] = jnp.full(m_ref.shape, neg_inf, F32)
    l_ref[...] = jnp.zeros(l_ref.shape, F32)
    acc_ref[...] = jnp.zeros_like(acc_ref)

    def logits_tile(kt):
        kv = key_rows(ckv_ref, kt)
        near = jnp.clip(kt - (R * qi - 1), -1, DSA_NEAR - 1) + 1
        mask = sc_ref[kt]
        for h in range(H):
            lg = jnp.dot(kv, dqt_ref[h], preferred_element_type=F32) + (mask + bias_ref[h, near])
            lg_ref[h, kt] = lg
            m_ref[h] = jnp.maximum(m_ref[h], fold(lg, jnp.max))

    for_tiles(logits_tile)
    m_row = [jnp.max(m_ref[h], axis=0, keepdims=True) for h in range(H)]

    def pv_tile(kt):
        kv = key_rows(ckv_ref, kt)
        for h in range(H):
            p = jnp.exp2(lg_ref[h, kt] - m_row[h])
            l_ref[h] += fold(p, jnp.sum)
            acc_ref[h] += _tn_dot(kv, p.astype(BF16))

    for_tiles(pv_tile)
    for h in range(H):
        out_t = acc_ref[h] / jnp.sum(l_ref[h], axis=0, keepdims=True)
        o_ref[0, :, h * LANES:(h + 1) * LANES] = out_t.T.astype(o_ref.dtype)


def _dsa(z, rel_bias, batch, seq):
    TQ, TK = DSA_TQ, DSA_TK
    nq, nk = seq // TQ, seq // TK
    top_k = min(TOPK_MAX, seq // 4)
    H = DSA_HEADS

    def r3(a):
        return a.reshape(batch, seq, a.shape[-1])

    misc = r3(z["misc"])
    return pl.pallas_call(
        functools.partial(_dsa_kernel, top_k=top_k),
        grid=(batch, nq),
        in_specs=[pl.BlockSpec((1, TQ, IDX_HEADS * LANES), lambda b, q: (b, q, 0)),
                  pl.BlockSpec((1, TQ, LANES), lambda b, q: (b, q, 0)),
                  pl.BlockSpec((1, seq, LANES), lambda b, q: (b, 0, 0)),
                  pl.BlockSpec((1, seq, DSA_LATENT), lambda b, q: (b, 0, 0)),
                  pl.BlockSpec((1, TQ, H * DSA_LATENT), lambda b, q: (b, q, 0)),
                  _const_spec((H, DSA_NEAR + 1, TK, TQ))],
        out_specs=pl.BlockSpec((1, TQ, H * DSA_LATENT), lambda b, q: (b, q, 0)),
        out_shape=jax.ShapeDtypeStruct((batch, seq, H * DSA_LATENT), BF16),
        scratch_shapes=[pltpu.VMEM((nk, TK, TQ), F32),
                        pltpu.VMEM((H, nk, TK, TQ), F32),
                        pltpu.VMEM((IDX_HEADS, LANES, TQ), BF16),
                        pltpu.VMEM((H, DSA_LATENT, TQ), BF16),
                        pltpu.VMEM((LANES, TQ), F32),
                        pltpu.VMEM((H, DSA_LATENT, TQ), F32),
                        pltpu.VMEM((2, H, SUBLANES, TQ), F32)],
        compiler_params=_params(("parallel", "arbitrary")),
        name="dsa",
    )(r3(z["iq"]), misc, misc, r3(z["ckv"]), r3(z["dq"]), _dsa_bias(rel_bias))


def _merge_kernel(h_ref, og_ref, od_ref, gt_ref, wa_ref, wd_ref, wo_ref, o_ref):
    d = h_ref.shape[-1]
    ya = jnp.dot(og_ref[...], wa_ref[...], preferred_element_type=F32)
    yd = jnp.dot(od_ref[...], wd_ref[...], preferred_element_type=F32)
    gt = gt_ref[...].astype(F32)
    mix = jax.nn.sigmoid(gt[:, :d]) * ya + jax.nn.sigmoid(gt[:, d:]) * yd
    o_ref[...] = h_ref[...] + jnp.dot(mix.astype(BF16), wo_ref[...], preferred_element_type=F32)


def _merge(h, og, od, gates, wa, wd, wo, tm):
    m, d = h.shape
    row = lambda w: pl.BlockSpec((tm, w), lambda i: (i, 0))
    return pl.pallas_call(
        _merge_kernel,
        grid=(m // tm,),
        in_specs=[row(d), row(og.shape[1]), row(od.shape[1]), row(2 * d),
                  _const_spec(wa.shape), _const_spec(wd.shape), _const_spec(wo.shape)],
        out_specs=row(d),
        out_shape=jax.ShapeDtypeStruct((m, d), F32),
        compiler_params=_params(("parallel",)),
        name="merge",
    )(h, og, od, gates, wa.astype(BF16), wd.astype(BF16), wo.astype(BF16))


def _ple_final_kernel(h_ref, p_ref, g_ref, wg_ref, wp_ref, fg_ref, o_ref):
    h = h_ref[...]
    hn = _rms(h, g_ref[...]).astype(BF16)
    gate = jax.nn.sigmoid(jnp.dot(hn, wg_ref[...], preferred_element_type=F32))
    emb = jnp.dot(p_ref[...].astype(BF16), wp_ref[...], preferred_element_type=F32)
    o_ref[...] = _rms(h + gate * emb, fg_ref[...])


def _ple_final(h, p, g, wg, wp, fg, tm):
    m, d = h.shape
    pd = p.shape[1]
    row = lambda w: pl.BlockSpec((tm, w), lambda i: (i, 0))
    return pl.pallas_call(
        _ple_final_kernel,
        grid=(m // tm,),
        in_specs=[row(d), row(pd), _const_spec((1, d)), _const_spec(wg.shape), _const_spec(wp.shape),
                  _const_spec((1, d))],
        out_specs=row(d),
        out_shape=jax.ShapeDtypeStruct((m, d), F32),
        compiler_params=_params(("parallel",)),
        name="ple_final",
    )(h, p, g.reshape(1, d), wg.astype(BF16), wp.astype(BF16), fg.reshape(1, d))


def kernel(x, p, ffn1_norm, ffn1_w_in, ffn1_w_out, mix_norm, mix_w_in, gla_alpha_w, gla_alpha_b, gla_out_norm, gla_w_out, ckv_norm, dsa_w_out, rel_bias, mix_w_out, ffn2_norm, ffn2_w_in, ffn2_w_out, ple_norm, ple_w_gate, ple_w_proj, final_norm):
    batch, seq, d = x.shape
    assert p.shape[0] == 1, "single-layer trunk"
    m = batch * seq
    tm = min(TOKEN_TILE, m)
    h = x.reshape(m, d)
    h = _ffn(h, ffn1_norm[0], ffn1_w_in[0], ffn1_w_out[0], tm)
    z = _inproj(h, mix_norm[0], mix_w_in[0], ckv_norm[0], tm)
    og = _gla(z, gla_alpha_w[0], gla_alpha_b[0], gla_out_norm[0], batch, seq)
    od = _dsa(z, rel_bias, batch, seq)
    h = _merge(h, og.reshape(m, -1), od.reshape(m, -1), z["gates"], gla_w_out[0], dsa_w_out[0], mix_w_out[0], tm)
    h = _ffn(h, ffn2_norm[0], ffn2_w_in[0], ffn2_w_out[0], tm)
    h = _ple_final(h, p[0].reshape(m, -1), ple_norm[0], ple_w_gate[0], ple_w_proj[0], final_norm, tm)
    return h.reshape(batch, seq, d)
```

```python
import functools
import math

import numpy as np
import jax
import jax.numpy as jnp
from jax import lax
from jax.experimental import pallas as pl
from jax.experimental.pallas import tpu as pltpu

F32 = jnp.float32
BF16 = jnp.bfloat16
HIGHEST = lax.Precision.HIGHEST

EPS = 1e-6
GLA_HEADS = 4
GLA_DK = 128
GLA_DV = 256
GLA_LOWRANK = 16
GLA_TAU = 16.0
DSA_HEADS = 8
DSA_LATENT = 128
IDX_HEADS = 8
IDX_DIM = 64
TOPK_MAX = 256
REL_BUCKETS = 32
REL_MAX_DIST = 128

LANES = 128
SUBLANES = 8
VMEM_LIMIT = 56 * 1024 * 1024
LOG2_E = math.log2(math.e)
F32_INF_BITS = 0x7F800000
F32_MAX = float(np.finfo(np.float32).max)

TOKEN_TILE = 512
PROJ_COLS = 512
FFN_COLS = 512
GLA_CHUNK = 128
GLA_SUB = 16
DSA_TQ = 256
DSA_TK = 256

MISC_IK = 0
MISC_GA = IDX_DIM
MISC_IW = IDX_DIM + GLA_LOWRANK


def _nt_dot(a, b):
    return lax.dot_general(a, b, (((1,), (1,)), ((), ())), preferred_element_type=F32)


def _tn_dot(a, b):
    return lax.dot_general(a, b, (((0,), (0,)), ((), ())), preferred_element_type=F32)


def _rms(x, g):
    return x * lax.rsqrt(jnp.mean(x * x, axis=-1, keepdims=True) + EPS) * g


def _const_spec(shape):
    nd = len(shape)
    return pl.BlockSpec(shape, lambda *_: (0,) * nd, pipeline_mode=pl.Buffered(1))


def _params(sem):
    return pltpu.CompilerParams(dimension_semantics=sem, vmem_limit_bytes=VMEM_LIMIT)


def _half_swiglu(x, g_ref, win_ref, wout_ref, act_ref, ff):
    xn = _rms(x, g_ref[...]).astype(BF16)
    for c0 in range(0, ff, FFN_COLS):
        cw = min(FFN_COLS, ff - c0)
        gate = jnp.dot(xn, win_ref[:, c0:c0 + cw], preferred_element_type=F32)
        up = jnp.dot(xn, win_ref[:, ff + c0:ff + c0 + cw], preferred_element_type=F32)
        act_ref[:, c0:c0 + cw] = (gate * jax.nn.sigmoid(gate) * up).astype(BF16)
    return x + 0.5 * jnp.dot(act_ref[...], wout_ref[...], preferred_element_type=F32)


def _ffn_kernel(h_ref, g_ref, win_ref, wout_ref, o_ref, act_ref, *, ff):
    o_ref[...] = _half_swiglu(h_ref[...], g_ref, win_ref, wout_ref, act_ref, ff)


def _ffn(h, g, w_in, w_out, tm):
    m, d = h.shape
    ff = w_out.shape[0]
    assert ff % LANES == 0
    return pl.pallas_call(
        functools.partial(_ffn_kernel, ff=ff),
        grid=(m // tm,),
        in_specs=[pl.BlockSpec((tm, d), lambda i: (i, 0)),
                  _const_spec((1, d)),
                  _const_spec((d, 2 * ff)),
                  _const_spec((ff, d))],
        out_specs=pl.BlockSpec((tm, d), lambda i: (i, 0)),
        out_shape=jax.ShapeDtypeStruct((m, d), F32),
        scratch_shapes=[pltpu.VMEM((tm, ff), BF16)],
        compiler_params=_params(("parallel",)),
        name="ffn",
    )(h, g.reshape(1, d), w_in.astype(BF16), w_out.astype(BF16))


_PROJ_OUT = (("gq", GLA_HEADS * GLA_DK, F32), ("gk", GLA_HEADS * GLA_DK, F32),
             ("gv", GLA_HEADS * GLA_DV, BF16), ("gr", GLA_HEADS * GLA_DV, BF16),
             ("dq", DSA_HEADS * DSA_LATENT, BF16), ("iq", IDX_HEADS * LANES, BF16),
             ("gates", None, BF16), ("ckv", DSA_LATENT, BF16), ("misc", LANES, F32))


def _inproj_kernel(h_ref, g_ref, ckvg_ref, wa_ref, wb_ref, *out_refs, widths):
    xn = _rms(h_ref[...], g_ref[...]).astype(BF16)
    w_ref, col = wa_ref, 0
    for (name, _, dt), width, o_ref in zip(_PROJ_OUT, widths, out_refs):
        if col == wa_ref.shape[1] and w_ref is wa_ref:
            w_ref, col = wb_ref, 0
        for c0 in range(0, width, PROJ_COLS):
            cw = min(PROJ_COLS, width - c0)
            z = jnp.dot(xn, w_ref[:, col + c0:col + c0 + cw], preferred_element_type=F32)
            if name == "ckv":
                z = _rms(z, ckvg_ref[...])
            elif name == "dq":
                z = z * (DSA_LATENT ** -0.5 * LOG2_E)
            o_ref[:, c0:c0 + cw] = z.astype(dt)
        col += width


def _inproj(h, g, w_in, ckv_g, tm):
    m, d = h.shape
    sizes = (GLA_HEADS * GLA_DK, GLA_HEADS * GLA_DK, GLA_HEADS * GLA_DV, GLA_HEADS * GLA_DV, GLA_LOWRANK,
             DSA_HEADS * DSA_LATENT, DSA_LATENT, IDX_HEADS * IDX_DIM, IDX_DIM, IDX_HEADS, 2 * d)
    pts = np.cumsum(sizes)[:-1].tolist()
    gq, gk, gv, gr, ga, dq, dkv, iq, ik, iw, gates = jnp.split(w_in, pts, axis=-1)
    iq = jnp.pad(iq.reshape(d, IDX_HEADS, IDX_DIM), ((0, 0), (0, 0), (0, LANES - IDX_DIM))).reshape(d, -1)
    misc = jnp.pad(jnp.concatenate([ik, ga, iw], axis=-1),
                   ((0, 0), (0, LANES - IDX_DIM - GLA_LOWRANK - IDX_HEADS)))
    wa = w_in[:, :pts[3]].astype(BF16)
    wb = jnp.concatenate([dq, iq, gates, dkv, misc], axis=-1).astype(BF16)
    widths = tuple(wd if wd is not None else 2 * d for _, wd, _ in _PROJ_OUT)
    assert wa.shape[1] == sum(widths[:4]) and wa.shape[1] + wb.shape[1] == sum(widths)
    outs = pl.pallas_call(
        functools.partial(_inproj_kernel, widths=widths),
        grid=(m // tm,),
        in_specs=[pl.BlockSpec((tm, d), lambda i: (i, 0)),
                  _const_spec((1, d)),
                  _const_spec((1, DSA_LATENT)),
                  _const_spec(wa.shape),
                  _const_spec(wb.shape)],
        out_specs=[pl.BlockSpec((tm, wd), lambda i: (i, 0)) for wd in widths],
        out_shape=[jax.ShapeDtypeStruct((m, wd), dt) for wd, (_, _, dt) in zip(widths, _PROJ_OUT)],
        compiler_params=_params(("parallel",)),
        name="inproj",
    )(h, g.reshape(1, d), ckv_g.reshape(1, DSA_LATENT), wa, wb)
    return dict(zip([n for n, _, _ in _PROJ_OUT], outs))


def _gla_scores_safe(qs, k, b, a_ref):
    C, SUB = GLA_CHUNK, GLA_SUB
    sub_t = lax.broadcasted_iota(jnp.int32, (SUB, 1), 0)
    out_lane = lax.broadcasted_iota(jnp.int32, (SUB, C), 1)
    for i in range(C // SUB):
        r0 = i * SUB
        b_i = b[r0:r0 + SUB]
        q_i = qs[r0:r0 + SUB]
        if i == 0:
            a_row = jnp.zeros((SUB, C), F32)
        else:
            beta = b[r0 - 1:r0]
            q_t = (q_i * jnp.exp(b_i - beta)).astype(BF16)
            k_t = (k[:r0] * jnp.exp(beta - b[:r0])).astype(BF16)
            k_t = jnp.concatenate([k_t, jnp.zeros((C - r0, GLA_DK), BF16)], axis=0)
            a_row = _nt_dot(q_t, k_t)
        for s in range(SUB):
            e = jnp.exp(jnp.where(sub_t >= s, b_i - b[r0 + s:r0 + s + 1], -jnp.inf))
            col = jnp.sum(q_i * k[r0 + s:r0 + s + 1] * e, axis=-1, keepdims=True)
            a_row = jnp.where(out_lane == r0 + s, col, a_row)
        a_ref[r0:r0 + SUB, :] = a_row


def _gla_kernel(q_ref, k_ref, v_ref, gr_ref, misc_ref, aw_ref, ab_ref, ng_ref, o_ref, st_ref, a_ref):
    C = GLA_CHUNK
    DK, DV = GLA_DK, GLA_DV

    @pl.when(pl.program_id(1) == 0)
    def _():
        st_ref[...] = jnp.zeros_like(st_ref)

    x = jnp.dot(misc_ref[0], aw_ref[...], precision=HIGHEST, preferred_element_type=F32) + ab_ref[...]
    la = (jnp.minimum(x, 0.0) - jnp.log1p(jnp.exp(-jnp.abs(x)))) / GLA_TAU
    row = lax.broadcasted_iota(jnp.int32, (C, C), 0)
    col = lax.broadcasted_iota(jnp.int32, (C, C), 1)
    causal = col <= row
    ones_tri = jnp.where(causal, 1.0, 0.0).astype(BF16)
    la_hi = la.astype(BF16)
    la_lo = (la - la_hi.astype(F32)).astype(BF16)
    b_all = (jnp.dot(ones_tri, la_hi, preferred_element_type=F32)
             + jnp.dot(ones_tri, la_lo, preferred_element_type=F32))

    def head(h):
        b = b_all[:, h * DK:(h + 1) * DK]
        k = k_ref[0, :, h * DK:(h + 1) * DK]
        qs = q_ref[0, :, h * DK:(h + 1) * DK] * (DK ** -0.5)
        return b, k, qs, qs * jnp.exp(b)

    bound = jnp.zeros((1, DK), F32)
    for h in range(GLA_HEADS):
        b, k, _, q_in = head(h)
        k_max = jnp.max(jnp.abs(k), axis=0, keepdims=True) * jnp.exp(-b[C - 1:C])
        bound = bound + jnp.maximum(jnp.max(jnp.abs(q_in), axis=0, keepdims=True), 1.0) * k_max
    factored_ok = jnp.sum(bound) < 1e30

    def finish(h, b, k, q_in, scores):
        v = v_ref[0, :, h * DV:(h + 1) * DV]
        st = st_ref[h]
        o = _nt_dot(q_in.astype(BF16), st.astype(BF16))
        o = o + jnp.dot(scores.astype(BF16), v, preferred_element_type=F32)
        o = _rms(o, ng_ref[:, h * DV:(h + 1) * DV])
        gr = gr_ref[0, :, h * DV:(h + 1) * DV].astype(F32)
        o_ref[0, :, h * DV:(h + 1) * DV] = (o * (gr * jax.nn.sigmoid(gr))).astype(o_ref.dtype)
        b_last = b[C - 1:C]
        k_dec = (k * jnp.exp(b_last - b)).astype(BF16)
        st_ref[h] = st * jnp.exp(b_last) + _tn_dot(v, k_dec)

    @pl.when(factored_ok)
    def _():
        for h in range(GLA_HEADS):
            b, k, _, q_in = head(h)
            k_out = k * jnp.exp(-b)
            scores = jnp.where(causal, _nt_dot(q_in.astype(BF16), k_out.astype(BF16)), 0.0)
            finish(h, b, k, q_in, scores)

    @pl.when(jnp.logical_not(factored_ok))
    def _():
        for h in range(GLA_HEADS):
            b, k, qs, q_in = head(h)
            _gla_scores_safe(qs, k, b, a_ref)
            finish(h, b, k, q_in, a_ref[...])


def _gla(z, alpha_w, alpha_b, norm_g, batch, seq):
    C = GLA_CHUNK
    H, DK, DV = GLA_HEADS, GLA_DK, GLA_DV

    def r3(a):
        return a.reshape(batch, seq, a.shape[-1])

    aw = jnp.zeros((LANES, H * DK), F32).at[MISC_GA:MISC_GA + GLA_LOWRANK, :].set(alpha_w)
    tok = lambda w: pl.BlockSpec((1, C, w), lambda b, c: (b, c, 0))
    return pl.pallas_call(
        _gla_kernel,
        grid=(batch, seq // C),
        in_specs=[tok(H * DK), tok(H * DK), tok(H * DV), tok(H * DV), tok(LANES),
                  _const_spec((LANES, H * DK)), _const_spec((1, H * DK)), _const_spec((1, H * DV))],
        out_specs=tok(H * DV),
        out_shape=jax.ShapeDtypeStruct((batch, seq, H * DV), BF16),
        scratch_shapes=[pltpu.VMEM((H, DV, DK), F32), pltpu.VMEM((C, C), F32)],
        compiler_params=_params(("parallel", "arbitrary")),
        name="gla",
    )(r3(z["gq"]), r3(z["gk"]), r3(z["gv"]), r3(z["gr"]), r3(z["misc"]),
      aw, alpha_b.reshape(1, H * DK), norm_g.reshape(1, H * DV))


DSA_NEAR = DSA_TQ // DSA_TK + 1


def _t5_bucket(dist):
    max_exact = REL_BUCKETS // 2
    d = np.maximum(dist, 0)
    steps = (np.log(np.maximum(d, 1).astype(np.float64) / max_exact) / math.log(REL_MAX_DIST / max_exact)
             * (REL_BUCKETS - max_exact))
    live = (d > max_exact) & (steps < REL_BUCKETS - max_exact - 0.5)
    assert np.all(np.abs(steps - np.round(steps))[live] > 1e-4)
    large = np.minimum(max_exact + steps.astype(np.int32), REL_BUCKETS - 1)
    return np.where(d < max_exact, d, large).astype(np.int32)


def _dsa_bias_kernel(rb_ref, bkt_ref, o_ref):
    h = pl.program_id(0)
    bkt = bkt_ref[...]
    far = rb_ref[REL_BUCKETS - 1, h]
    acc = jnp.zeros(bkt.shape, F32)
    for j in range(REL_BUCKETS):
        acc = jnp.where(bkt == j, rb_ref[j, h] - far, acc)
    o_ref[0] = acc * LOG2_E


def _dsa_bias(rel_bias):
    assert DSA_TK + 1 >= REL_MAX_DIST
    s = np.arange(DSA_TK, dtype=np.int32)[:, None]
    t = np.arange(DSA_TQ, dtype=np.int32)[None, :]
    near = [_t5_bucket(t - s + DSA_TK * (1 - j)) for j in range(DSA_NEAR)]
    bkt = jnp.asarray(np.stack([np.full((DSA_TK, DSA_TQ), -1, np.int32)] + near))
    n = DSA_NEAR + 1
    return pl.pallas_call(
        _dsa_bias_kernel,
        grid=(DSA_HEADS,),
        in_specs=[pl.BlockSpec(memory_space=pltpu.SMEM),
                  pl.BlockSpec((n, DSA_TK, DSA_TQ), lambda h: (0, 0, 0))],
        out_specs=pl.BlockSpec((1, n, DSA_TK, DSA_TQ), lambda h: (h, 0, 0, 0)),
        out_shape=jax.ShapeDtypeStruct((DSA_HEADS, n, DSA_TK, DSA_TQ), F32),
        name="dsa_bias",
    )(rel_bias, bkt)


def _dsa_kernel(iq_ref, mq_ref, mk_ref, ckv_ref, dq_ref, bias_ref, o_ref,
                sc_ref, scb_ref, lg_ref, iqt_ref, dqt_ref, wt_ref, acc_ref, ml_ref, *, top_k):
    TQ, TK = DSA_TQ, DSA_TK
    R = TQ // TK
    H = DSA_HEADS
    qi = pl.program_id(1)
    ngroups = qi + 1
    neg_inf = -jnp.inf
    kf = float(top_k)

    wt_ref[...] = mq_ref[0].T * ((IDX_HEADS ** -0.5) * (IDX_DIM ** -0.5))
    for h in range(H):
        iqt_ref[h] = iq_ref[0, :, h * LANES:(h + 1) * LANES].T
        dqt_ref[h] = dq_ref[0, :, h * LANES:(h + 1) * LANES].T

    def key_rows(ref, kt):
        return ref[0, pl.ds(pl.multiple_of(kt * TK, TK), TK), :]

    def fold(x, op):
        return op(x.reshape(TK // SUBLANES, SUBLANES, TQ), axis=0)

    k_pos = lax.broadcasted_iota(jnp.int32, (TK, TQ), 0)
    q_pos = lax.broadcasted_iota(jnp.int32, (TK, TQ), 1) + qi * TQ

    def for_tiles(tile_fn):
        def pair(g, carry):
            tile_fn(2 * g)
            tile_fn(2 * g + 1)
            return carry

        lax.fori_loop(0, ngroups // 2, pair, 0)

        @pl.when(ngroups % 2 == 1)
        def _():
            tile_fn(ngroups - 1)

    def score_tile(kt):
        keys = key_rows(mk_ref, kt).astype(BF16)
        acc = jnp.zeros((TK, TQ), F32)
        for h in range(IDX_HEADS):
            d = jnp.dot(keys, iqt_ref[h], preferred_element_type=F32)
            acc = acc + jnp.maximum(d, 0.0) * wt_ref[MISC_IW + h:MISC_IW + h + 1, :]
        sc = jnp.where(k_pos + kt * TK <= q_pos, acc, neg_inf)
        sc_ref[kt] = sc
        scb_ref[kt] = sc.astype(BF16)

    for_tiles(score_tile)

    rows = 4 * SUBLANES

    def count_tiles(hits, dtype):
        def pair(g, acc):
            return acc + hits(2 * g) + hits(2 * g + 1)

        acc = lax.fori_loop(0, ngroups // 2, pair, jnp.zeros((rows, TQ), dtype))
        acc = lax.cond(ngroups % 2 == 1, lambda a: a + hits(ngroups - 1), lambda a: a, acc)
        return jnp.sum(acc.astype(F32), axis=0, keepdims=True)

    def count_ge(cand):
        def hits(kt):
            hit = jnp.where(sc_ref[kt] >= cand, 1.0, 0.0)
            return jnp.sum(hit.reshape(TK // rows, rows, TQ), axis=0)

        return count_tiles(hits, F32)

    def count_ge_rounded(cand):
        cand = cand.astype(BF16)

        def hits(kt):
            hit = jnp.where(scb_ref[kt] >= cand, jnp.ones((), BF16), jnp.zeros((), BF16))
            acc = hit[:rows]
            for i in range(1, TK // rows):
                acc = acc + hit[i * rows:(i + 1) * rows]
            return acc

        return count_tiles(hits, BF16)

    def val_of(key):
        return pltpu.bitcast(key ^ ((key >> 31) & 0x7FFFFFFF), F32)

    def bisect(steps, count, value, lo, hi):
        def step(_, state):
            lo, hi, n_lo = state
            mid = lo + ((hi - lo) >> 1)
            n_mid = count(value(mid))
            ok = n_mid >= kf
            return jnp.where(ok, mid, lo), jnp.where(ok, hi, mid), jnp.where(ok, n_mid, n_lo)

        lo, _, n_lo = lax.fori_loop(0, steps, step, (lo, hi, jnp.full((1, TQ), kf, F32)))
        return lo, n_lo

    key16_inf = F32_INF_BITS >> 16
    coarse, _ = bisect(16, count_ge_rounded, lambda k16: val_of((k16 << 16) | ((k16 >> 31) & 0xFFFF)),
                       jnp.full((1, TQ), -key16_inf - 1, jnp.int32), jnp.full((1, TQ), key16_inf + 1, jnp.int32))
    key_inf = F32_INF_BITS
    fine_lo = jnp.maximum((coarse - 1) << 16, -key_inf - 1)
    fine_hi = jnp.minimum(((coarse + 2) << 16) | 0xFFFF, key_inf) + 1
    key_thr, n_ge = bisect(18, count_ge, val_of, fine_lo, fine_hi)
    thr = jnp.maximum(val_of(key_thr), -F32_MAX)
    has_ties = jnp.max(n_ge) > kf

    @pl.when(jnp.logical_not(has_ties))
    def _():
        def body(g, carry):
            for u in range(R):
                kt = g * R + u
                sc_ref[kt] = jnp.where(sc_ref[kt] >= thr, 0.0, neg_inf)
            return carry

        lax.fori_loop(0, ngroups, body, 0)

    @pl.when(has_ties)
    def _():
        def count_gt(g, acc):
            for u in range(R):
                acc = acc + fold(jnp.where(sc_ref[g * R + u] > thr, 1.0, 0.0), jnp.sum)
            return acc

        n_gt = lax.fori_loop(0, ngroups, count_gt, jnp.zeros((SUBLANES, TQ), F32))
        need = kf - jnp.sum(n_gt, axis=0, keepdims=True)
        incl = jnp.where(lax.broadcasted_iota(jnp.int32, (TK, TK), 1)
                         <= lax.broadcasted_iota(jnp.int32, (TK, TK), 0), 1.0, 0.0).astype(BF16)

        def body(g, seen):
            for u in range(R):
                kt = g * R + u
                sc = sc_ref[kt]
                eq = jnp.where(sc == thr, 1.0, 0.0)
                rank = seen + jnp.dot(incl, eq.astype(BF16), preferred_element_type=F32)
                keep = jnp.where(sc > thr, 1.0, jnp.where(rank <= need, eq, 0.0))
                sc_ref[kt] = jnp.where(keep > 0.0, 0.0, neg_inf)
                seen = seen + jnp.sum(eq, axis=0, keepdims=True)
            return seen

        lax.fori_loop(0, ngroups, body, jnp.zeros((1, TQ), F32))

    m_ref, l_ref = ml_ref.at[0], ml_ref.at[1]
    m_ref[...] = jnp.full(m_ref.shape, neg_inf, F32)
    l_ref[...] = jnp.zeros(l_ref.shape, F32)
    acc_ref[...] = jnp.zeros_like(acc_ref)

    def logits_tile(kt):
        kv = key_rows(ckv_ref, kt)
        near = jnp.clip(kt - (R * qi - 1), -1, DSA_NEAR - 1) + 1
        mask = sc_ref[kt]
        for h in range(H):
            lg = jnp.dot(kv, dqt_ref[h], preferred_element_type=F32) + (mask + bias_ref[h, near])
            lg_ref[h, kt] = lg
            m_ref[h] = jnp.maximum(m_ref[h], fold(lg, jnp.max))

    for_tiles(logits_tile)
    m_row = [jnp.max(m_ref[h], axis=0, keepdims=True) for h in range(H)]

    def pv_tile(kt):
        kv = key_rows(ckv_ref, kt)
        for h in range(H):
            p = jnp.exp2(lg_ref[h, kt] - m_row[h])
            l_ref[h] += fold(p, jnp.sum)
            acc_ref[h] += _tn_dot(kv, p.astype(BF16))

    for_tiles(pv_tile)
    for h in range(H):
        out_t = acc_ref[h] / jnp.sum(l_ref[h], axis=0, keepdims=True)
        o_ref[0, :, h * LANES:(h + 1) * LANES] = out_t.T.astype(o_ref.dtype)


def _dsa(z, rel_bias, batch, seq):
    TQ, TK = DSA_TQ, DSA_TK
    nq, nk = seq // TQ, seq // TK
    top_k = min(TOPK_MAX, seq // 4)
    H = DSA_HEADS

    def r3(a):
        return a.reshape(batch, seq, a.shape[-1])

    misc = r3(z["misc"])
    return pl.pallas_call(
        functools.partial(_dsa_kernel, top_k=top_k),
        grid=(batch, nq),
        in_specs=[pl.BlockSpec((1, TQ, IDX_HEADS * LANES), lambda b, q: (b, q, 0)),
                  pl.BlockSpec((1, TQ, LANES), lambda b, q: (b, q, 0)),
                  pl.BlockSpec((1, seq, LANES), lambda b, q: (b, 0, 0)),
                  pl.BlockSpec((1, seq, DSA_LATENT), lambda b, q: (b, 0, 0)),
                  pl.BlockSpec((1, TQ, H * DSA_LATENT), lambda b, q: (b, q, 0)),
                  _const_spec((H, DSA_NEAR + 1, TK, TQ))],
        out_specs=pl.BlockSpec((1, TQ, H * DSA_LATENT), lambda b, q: (b, q, 0)),
        out_shape=jax.ShapeDtypeStruct((batch, seq, H * DSA_LATENT), BF16),
        scratch_shapes=[pltpu.VMEM((nk, TK, TQ), F32),
                        pltpu.VMEM((nk, TK, TQ), BF16),
                        pltpu.VMEM((H, nk, TK, TQ), F32),
                        pltpu.VMEM((IDX_HEADS, LANES, TQ), BF16),
                        pltpu.VMEM((H, DSA_LATENT, TQ), BF16),
                        pltpu.VMEM((LANES, TQ), F32),
                        pltpu.VMEM((H, DSA_LATENT, TQ), F32),
                        pltpu.VMEM((2, H, SUBLANES, TQ), F32)],
        compiler_params=_params(("parallel", "arbitrary")),
        name="dsa",
    )(r3(z["iq"]), misc, misc, r3(z["ckv"]), r3(z["dq"]), _dsa_bias(rel_bias))


def _tail_kernel(h_ref, og_ref, od_ref, gt_ref, p_ref, wa_ref, wd_ref, wo_ref, g2_ref, win_ref, wout_ref,
                 gp_ref, wg_ref, wp_ref, gf_ref, o_ref, act_ref, *, ff):
    d = h_ref.shape[-1]
    ya = jnp.dot(og_ref[...], wa_ref[...], preferred_element_type=F32)
    yd = jnp.dot(od_ref[...], wd_ref[...], preferred_element_type=F32)
    gt = gt_ref[...].astype(F32)
    mix = jax.nn.sigmoid(gt[:, :d]) * ya + jax.nn.sigmoid(gt[:, d:]) * yd
    h = h_ref[...] + jnp.dot(mix.astype(BF16), wo_ref[...], preferred_element_type=F32)
    h = _half_swiglu(h, g2_ref, win_ref, wout_ref, act_ref, ff)
    hn = _rms(h, gp_ref[...]).astype(BF16)
    gate = jax.nn.sigmoid(jnp.dot(hn, wg_ref[...], preferred_element_type=F32))
    emb = jnp.dot(p_ref[...].astype(BF16), wp_ref[...], preferred_element_type=F32)
    o_ref[...] = _rms(h + gate * emb, gf_ref[...])


def _tail(h, og, od, gates, p, wa, wd, wo, g2, w_in, w_out, gp, wg, wp, gf, tm):
    m, d = h.shape
    ff = w_out.shape[0]
    row = lambda w: pl.BlockSpec((tm, w), lambda i: (i, 0))
    vec = lambda g: g.reshape(1, d)
    weights = [w.astype(BF16) for w in (wa, wd, wo)] + [vec(g2), w_in.astype(BF16), w_out.astype(BF16),
                                                         vec(gp), wg.astype(BF16), wp.astype(BF16), vec(gf)]
    return pl.pallas_call(
        functools.partial(_tail_kernel, ff=ff),
        grid=(m // tm,),
        in_specs=[row(d), row(og.shape[1]), row(od.shape[1]), row(2 * d), row(p.shape[1])]
                 + [_const_spec(w.shape) for w in weights],
        out_specs=row(d),
        out_shape=jax.ShapeDtypeStruct((m, d), F32),
        scratch_shapes=[pltpu.VMEM((tm, ff), BF16)],
        compiler_params=_params(("parallel",)),
        name="tail",
    )(h, og, od, gates, p, *weights)


def kernel(x, p, ffn1_norm, ffn1_w_in, ffn1_w_out, mix_norm, mix_w_in, gla_alpha_w, gla_alpha_b, gla_out_norm, gla_w_out, ckv_norm, dsa_w_out, rel_bias, mix_w_out, ffn2_norm, ffn2_w_in, ffn2_w_out, ple_norm, ple_w_gate, ple_w_proj, final_norm):
    batch, seq, d = x.shape
    assert p.shape[0] == 1, "single-layer trunk"
    m = batch * seq
    tm = min(TOKEN_TILE, m)
    h = x.reshape(m, d)
    h = _ffn(h, ffn1_norm[0], ffn1_w_in[0], ffn1_w_out[0], tm)
    z = _inproj(h, mix_norm[0], mix_w_in[0], ckv_norm[0], tm)
    og = _gla(z, gla_alpha_w[0], gla_alpha_b[0], gla_out_norm[0], batch, seq)
    od = _dsa(z, rel_bias, batch, seq)
    h = _tail(h, og.reshape(m, -1), od.reshape(m, -1), z["gates"], p[0].reshape(m, -1),
              gla_w_out[0], dsa_w_out[0], mix_w_out[0], ffn2_norm[0], ffn2_w_in[0], ffn2_w_out[0],
              ple_norm[0], ple_w_gate[0], ple_w_proj[0], final_norm, tm)
    return h.reshape(batch, seq, d)
```

```python
import functools
import math

import numpy as np
import jax
import jax.numpy as jnp
from jax import lax
from jax.experimental import pallas as pl
from jax.experimental.pallas import tpu as pltpu

F32 = jnp.float32
BF16 = jnp.bfloat16

EPS = 1e-6
GLA_HEADS = 4
GLA_DK = 128
GLA_DV = 256
GLA_LOWRANK = 16
GLA_TAU = 16.0
DSA_HEADS = 8
DSA_LATENT = 128
IDX_HEADS = 8
IDX_DIM = 64
TOPK_MAX = 256
REL_BUCKETS = 32
REL_MAX_DIST = 128

LANES = 128
SUBLANES = 8
VMEM_LIMIT = 56 * 1024 * 1024
LOG2_E = math.log2(math.e)
F32_INF_BITS = 0x7F800000
F32_MAX = float(np.finfo(np.float32).max)

TOKEN_TILE = 512
PROJ_COLS = 512
FFN_COLS = 512
GLA_CHUNK = 128
GLA_SUB = 16
DSA_TQ = 256
DSA_TK = 256

MISC_IK = 0
MISC_GA = IDX_DIM
MISC_IW = IDX_DIM + GLA_LOWRANK


def _nt_dot(a, b):
    return lax.dot_general(a, b, (((1,), (1,)), ((), ())), preferred_element_type=F32)


def _tn_dot(a, b):
    return lax.dot_general(a, b, (((0,), (0,)), ((), ())), preferred_element_type=F32)


def _rms(x, g):
    return x * lax.rsqrt(jnp.mean(x * x, axis=-1, keepdims=True) + EPS) * g


def _bf16_pieces(x):
    hi = x.astype(BF16)
    return hi, (x - hi.astype(F32)).astype(BF16)


def _dot_two_piece(a, b):
    a_hi, a_lo = _bf16_pieces(a)
    b_hi, b_lo = _bf16_pieces(b)
    dot = functools.partial(jnp.dot, preferred_element_type=F32)
    return dot(a_hi, b_hi) + dot(a_lo, b_hi) + dot(a_hi, b_lo)


def _const_spec(shape):
    nd = len(shape)
    return pl.BlockSpec(shape, lambda *_: (0,) * nd, pipeline_mode=pl.Buffered(1))


def _params(sem):
    return pltpu.CompilerParams(dimension_semantics=sem, vmem_limit_bytes=VMEM_LIMIT)


def _half_swiglu(x, g_ref, win_ref, wout_ref, act_ref, ff):
    xn = _rms(x, g_ref[...]).astype(BF16)
    for c0 in range(0, ff, FFN_COLS):
        cw = min(FFN_COLS, ff - c0)
        gate = jnp.dot(xn, win_ref[:, c0:c0 + cw], preferred_element_type=F32)
        up = jnp.dot(xn, win_ref[:, ff + c0:ff + c0 + cw], preferred_element_type=F32)
        act_ref[:, c0:c0 + cw] = (gate * jax.nn.sigmoid(gate) * up).astype(BF16)
    return x + 0.5 * jnp.dot(act_ref[...], wout_ref[...], preferred_element_type=F32)


def _ffn_kernel(h_ref, g_ref, win_ref, wout_ref, o_ref, act_ref, *, ff):
    o_ref[...] = _half_swiglu(h_ref[...], g_ref, win_ref, wout_ref, act_ref, ff)


def _ffn(h, g, w_in, w_out, tm):
    m, d = h.shape
    ff = w_out.shape[0]
    assert ff % LANES == 0
    return pl.pallas_call(
        functools.partial(_ffn_kernel, ff=ff),
        grid=(m // tm,),
        in_specs=[pl.BlockSpec((tm, d), lambda i: (i, 0)),
                  _const_spec((1, d)),
                  _const_spec((d, 2 * ff)),
                  _const_spec((ff, d))],
        out_specs=pl.BlockSpec((tm, d), lambda i: (i, 0)),
        out_shape=jax.ShapeDtypeStruct((m, d), F32),
        scratch_shapes=[pltpu.VMEM((tm, ff), BF16)],
        compiler_params=_params(("parallel",)),
        name="ffn",
    )(h, g.reshape(1, d), w_in.astype(BF16), w_out.astype(BF16))


_PROJ_OUT = (("gq", GLA_HEADS * GLA_DK, F32), ("gk", GLA_HEADS * GLA_DK, F32),
             ("gv", GLA_HEADS * GLA_DV, BF16), ("gr", GLA_HEADS * GLA_DV, BF16),
             ("dq", DSA_HEADS * DSA_LATENT, BF16), ("iq", IDX_HEADS * LANES, BF16),
             ("gates", None, BF16), ("ckv", DSA_LATENT, BF16), ("misc", LANES, F32))


def _inproj_kernel(h_ref, g_ref, ckvg_ref, wa_ref, wb_ref, *out_refs, widths):
    xn = _rms(h_ref[...], g_ref[...]).astype(BF16)
    w_ref, col = wa_ref, 0
    for (name, _, dt), width, o_ref in zip(_PROJ_OUT, widths, out_refs):
        if col == wa_ref.shape[1] and w_ref is wa_ref:
            w_ref, col = wb_ref, 0
        for c0 in range(0, width, PROJ_COLS):
            cw = min(PROJ_COLS, width - c0)
            z = jnp.dot(xn, w_ref[:, col + c0:col + c0 + cw], preferred_element_type=F32)
            if name == "ckv":
                z = _rms(z, ckvg_ref[...])
            elif name == "dq":
                z = z * (DSA_LATENT ** -0.5 * LOG2_E)
            o_ref[:, c0:c0 + cw] = z.astype(dt)
        col += width


def _inproj(h, g, w_in, ckv_g, tm):
    m, d = h.shape
    sizes = (GLA_HEADS * GLA_DK, GLA_HEADS * GLA_DK, GLA_HEADS * GLA_DV, GLA_HEADS * GLA_DV, GLA_LOWRANK,
             DSA_HEADS * DSA_LATENT, DSA_LATENT, IDX_HEADS * IDX_DIM, IDX_DIM, IDX_HEADS, 2 * d)
    pts = np.cumsum(sizes)[:-1].tolist()
    gq, gk, gv, gr, ga, dq, dkv, iq, ik, iw, gates = jnp.split(w_in, pts, axis=-1)
    iq = jnp.pad(iq.reshape(d, IDX_HEADS, IDX_DIM), ((0, 0), (0, 0), (0, LANES - IDX_DIM))).reshape(d, -1)
    misc = jnp.pad(jnp.concatenate([ik, ga, iw], axis=-1),
                   ((0, 0), (0, LANES - IDX_DIM - GLA_LOWRANK - IDX_HEADS)))
    wa = w_in[:, :pts[3]].astype(BF16)
    wb = jnp.concatenate([dq, iq, gates, dkv, misc], axis=-1).astype(BF16)
    widths = tuple(wd if wd is not None else 2 * d for _, wd, _ in _PROJ_OUT)
    assert wa.shape[1] == sum(widths[:4]) and wa.shape[1] + wb.shape[1] == sum(widths)
    outs = pl.pallas_call(
        functools.partial(_inproj_kernel, widths=widths),
        grid=(m // tm,),
        in_specs=[pl.BlockSpec((tm, d), lambda i: (i, 0)),
                  _const_spec((1, d)),
                  _const_spec((1, DSA_LATENT)),
                  _const_spec(wa.shape),
                  _const_spec(wb.shape)],
        out_specs=[pl.BlockSpec((tm, wd), lambda i: (i, 0)) for wd in widths],
        out_shape=[jax.ShapeDtypeStruct((m, wd), dt) for wd, (_, _, dt) in zip(widths, _PROJ_OUT)],
        compiler_params=_params(("parallel",)),
        name="inproj",
    )(h, g.reshape(1, d), ckv_g.reshape(1, DSA_LATENT), wa, wb)
    return dict(zip([n for n, _, _ in _PROJ_OUT], outs))


def _gla_scores_safe(qs, k, b, a_ref):
    C, SUB = GLA_CHUNK, GLA_SUB
    sub_t = lax.broadcasted_iota(jnp.int32, (SUB, 1), 0)
    out_lane = lax.broadcasted_iota(jnp.int32, (SUB, C), 1)
    for i in range(C // SUB):
        r0 = i * SUB
        b_i = b[r0:r0 + SUB]
        q_i = qs[r0:r0 + SUB]
        if i == 0:
            a_row = jnp.zeros((SUB, C), F32)
        else:
            beta = b[r0 - 1:r0]
            q_t = (q_i * jnp.exp(b_i - beta)).astype(BF16)
            k_t = (k[:r0] * jnp.exp(beta - b[:r0])).astype(BF16)
            k_t = jnp.concatenate([k_t, jnp.zeros((C - r0, GLA_DK), BF16)], axis=0)
            a_row = _nt_dot(q_t, k_t)
        for s in range(SUB):
            e = jnp.exp(jnp.where(sub_t >= s, b_i - b[r0 + s:r0 + s + 1], -jnp.inf))
            col = jnp.sum(q_i * k[r0 + s:r0 + s + 1] * e, axis=-1, keepdims=True)
            a_row = jnp.where(out_lane == r0 + s, col, a_row)
        a_ref[r0:r0 + SUB, :] = a_row


def _gla_kernel(q_ref, k_ref, v_ref, gr_ref, misc_ref, aw_ref, ab_ref, ng_ref, o_ref, st_ref, a_ref):
    C = GLA_CHUNK
    DK, DV = GLA_DK, GLA_DV

    @pl.when(pl.program_id(1) == 0)
    def _():
        st_ref[...] = jnp.zeros_like(st_ref)

    x = _dot_two_piece(misc_ref[0], aw_ref[...]) + ab_ref[...]
    la = (jnp.minimum(x, 0.0) - jnp.log1p(jnp.exp(-jnp.abs(x)))) / GLA_TAU
    row = lax.broadcasted_iota(jnp.int32, (C, C), 0)
    col = lax.broadcasted_iota(jnp.int32, (C, C), 1)
    causal = col <= row
    ones_tri = jnp.where(causal, 1.0, 0.0).astype(BF16)
    la_hi, la_lo = _bf16_pieces(la)
    b_all = (jnp.dot(ones_tri, la_hi, preferred_element_type=F32)
             + jnp.dot(ones_tri, la_lo, preferred_element_type=F32))

    def head(h):
        b = b_all[:, h * DK:(h + 1) * DK]
        k = k_ref[0, :, h * DK:(h + 1) * DK]
        qs = q_ref[0, :, h * DK:(h + 1) * DK] * (DK ** -0.5)
        return b, k, qs, qs * jnp.exp(b)

    bound = jnp.zeros((1, DK), F32)
    for h in range(GLA_HEADS):
        b, k, _, q_in = head(h)
        k_max = jnp.max(jnp.abs(k), axis=0, keepdims=True) * jnp.exp(-b[C - 1:C])
        bound = bound + jnp.maximum(jnp.max(jnp.abs(q_in), axis=0, keepdims=True), 1.0) * k_max
    factored_ok = jnp.sum(bound) < 1e30

    def finish(h, b, k, q_in, scores):
        v = v_ref[0, :, h * DV:(h + 1) * DV]
        st = st_ref[h]
        o = _nt_dot(q_in.astype(BF16), st.astype(BF16))
        o = o + jnp.dot(scores.astype(BF16), v, preferred_element_type=F32)
        o = _rms(o, ng_ref[:, h * DV:(h + 1) * DV])
        gr = gr_ref[0, :, h * DV:(h + 1) * DV].astype(F32)
        o_ref[0, :, h * DV:(h + 1) * DV] = (o * (gr * jax.nn.sigmoid(gr))).astype(o_ref.dtype)
        b_last = b[C - 1:C]
        k_dec = (k * jnp.exp(b_last - b)).astype(BF16)
        st_ref[h] = st * jnp.exp(b_last) + _tn_dot(v, k_dec)

    @pl.when(factored_ok)
    def _():
        for h in range(GLA_HEADS):
            b, k, _, q_in = head(h)
            k_out = k * jnp.exp(-b)
            scores = jnp.where(causal, _nt_dot(q_in.astype(BF16), k_out.astype(BF16)), 0.0)
            finish(h, b, k, q_in, scores)

    @pl.when(jnp.logical_not(factored_ok))
    def _():
        for h in range(GLA_HEADS):
            b, k, qs, q_in = head(h)
            _gla_scores_safe(qs, k, b, a_ref)
            finish(h, b, k, q_in, a_ref[...])


def _gla(z, alpha_w, alpha_b, norm_g, batch, seq):
    C = GLA_CHUNK
    H, DK, DV = GLA_HEADS, GLA_DK, GLA_DV

    def r3(a):
        return a.reshape(batch, seq, a.shape[-1])

    aw = jnp.zeros((LANES, H * DK), F32).at[MISC_GA:MISC_GA + GLA_LOWRANK, :].set(alpha_w)
    tok = lambda w: pl.BlockSpec((1, C, w), lambda b, c: (b, c, 0))
    return pl.pallas_call(
        _gla_kernel,
        grid=(batch, seq // C),
        in_specs=[tok(H * DK), tok(H * DK), tok(H * DV), tok(H * DV), tok(LANES),
                  _const_spec((LANES, H * DK)), _const_spec((1, H * DK)), _const_spec((1, H * DV))],
        out_specs=tok(H * DV),
        out_shape=jax.ShapeDtypeStruct((batch, seq, H * DV), BF16),
        scratch_shapes=[pltpu.VMEM((H, DV, DK), F32), pltpu.VMEM((C, C), F32)],
        compiler_params=_params(("parallel", "arbitrary")),
        name="gla",
    )(r3(z["gq"]), r3(z["gk"]), r3(z["gv"]), r3(z["gr"]), r3(z["misc"]),
      aw, alpha_b.reshape(1, H * DK), norm_g.reshape(1, H * DV))


DSA_NEAR = DSA_TQ // DSA_TK + 1


def _t5_bucket(dist):
    max_exact = REL_BUCKETS // 2
    d = np.maximum(dist, 0)
    steps = (np.log(np.maximum(d, 1).astype(np.float64) / max_exact) / math.log(REL_MAX_DIST / max_exact)
             * (REL_BUCKETS - max_exact))
    live = (d > max_exact) & (steps < REL_BUCKETS - max_exact - 0.5)
    assert np.all(np.abs(steps - np.round(steps))[live] > 1e-4)
    large = np.minimum(max_exact + steps.astype(np.int32), REL_BUCKETS - 1)
    return np.where(d < max_exact, d, large).astype(np.int32)


def _dsa_bias_kernel(rb_ref, bkt_ref, o_ref):
    h = pl.program_id(0)
    bkt = bkt_ref[...]
    far = rb_ref[REL_BUCKETS - 1, h]
    acc = jnp.zeros(bkt.shape, F32)
    for j in range(REL_BUCKETS):
        acc = jnp.where(bkt == j, rb_ref[j, h] - far, acc)
    o_ref[0] = acc * LOG2_E


def _dsa_bias(rel_bias):
    assert DSA_TK + 1 >= REL_MAX_DIST
    s = np.arange(DSA_TK, dtype=np.int32)[:, None]
    t = np.arange(DSA_TQ, dtype=np.int32)[None, :]
    near = [_t5_bucket(t - s + DSA_TK * (1 - j)) for j in range(DSA_NEAR)]
    bkt = jnp.asarray(np.stack([np.full((DSA_TK, DSA_TQ), -1, np.int32)] + near))
    n = DSA_NEAR + 1
    return pl.pallas_call(
        _dsa_bias_kernel,
        grid=(DSA_HEADS,),
        in_specs=[pl.BlockSpec(memory_space=pltpu.SMEM),
                  pl.BlockSpec((n, DSA_TK, DSA_TQ), lambda h: (0, 0, 0))],
        out_specs=pl.BlockSpec((1, n, DSA_TK, DSA_TQ), lambda h: (h, 0, 0, 0)),
        out_shape=jax.ShapeDtypeStruct((DSA_HEADS, n, DSA_TK, DSA_TQ), F32),
        name="dsa_bias",
    )(rel_bias, bkt)


def _dsa_kernel(iq_ref, mq_ref, mk_ref, ckv_ref, dq_ref, bias_ref, o_ref,
                sc_ref, scb_ref, lg_ref, iqt_ref, dqt_ref, wt_ref, acc_ref, ml_ref, *, top_k):
    TQ, TK = DSA_TQ, DSA_TK
    R = TQ // TK
    H = DSA_HEADS
    qi = pl.program_id(1)
    ngroups = qi + 1
    neg_inf = -jnp.inf
    kf = float(top_k)

    wt_ref[...] = mq_ref[0].T * ((IDX_HEADS ** -0.5) * (IDX_DIM ** -0.5))
    for h in range(H):
        iqt_ref[h] = iq_ref[0, :, h * LANES:(h + 1) * LANES].T
        dqt_ref[h] = dq_ref[0, :, h * LANES:(h + 1) * LANES].T

    def key_rows(ref, kt):
        return ref[0, pl.ds(pl.multiple_of(kt * TK, TK), TK), :]

    def fold(x, op):
        return op(x.reshape(TK // SUBLANES, SUBLANES, TQ), axis=0)

    k_pos = lax.broadcasted_iota(jnp.int32, (TK, TQ), 0)
    q_pos = lax.broadcasted_iota(jnp.int32, (TK, TQ), 1) + qi * TQ

    def for_tiles(tile_fn):
        def pair(g, carry):
            tile_fn(2 * g)
            tile_fn(2 * g + 1)
            return carry

        lax.fori_loop(0, ngroups // 2, pair, 0)

        @pl.when(ngroups % 2 == 1)
        def _():
            tile_fn(ngroups - 1)

    def score_tile(kt):
        keys = key_rows(mk_ref, kt).astype(BF16)
        acc = jnp.zeros((TK, TQ), F32)
        for h in range(IDX_HEADS):
            d = jnp.dot(keys, iqt_ref[h], preferred_element_type=F32)
            acc = acc + jnp.maximum(d, 0.0) * wt_ref[MISC_IW + h:MISC_IW + h + 1, :]
        sc = jnp.where(k_pos + kt * TK <= q_pos, acc, neg_inf)
        sc_ref[kt] = sc
        scb_ref[kt] = sc.astype(BF16)

    for_tiles(score_tile)

    rows = 4 * SUBLANES

    def count_tiles(hits, dtype):
        def pair(g, acc):
            return acc + hits(2 * g) + hits(2 * g + 1)

        acc = lax.fori_loop(0, ngroups // 2, pair, jnp.zeros((rows, TQ), dtype))
        acc = lax.cond(ngroups % 2 == 1, lambda a: a + hits(ngroups - 1), lambda a: a, acc)
        return jnp.sum(acc.astype(F32), axis=0, keepdims=True)

    def count_ge(cand):
        def hits(kt):
            hit = jnp.where(sc_ref[kt] >= cand, 1.0, 0.0)
            return jnp.sum(hit.reshape(TK // rows, rows, TQ), axis=0)

        return count_tiles(hits, F32)

    def count_ge_rounded(cand):
        cand = cand.astype(BF16)

        def hits(kt):
            hit = jnp.where(scb_ref[kt] >= cand, jnp.ones((), BF16), jnp.zeros((), BF16))
            acc = hit[:rows]
            for i in range(1, TK // rows):
                acc = acc + hit[i * rows:(i + 1) * rows]
            return acc

        return count_tiles(hits, BF16)

    def val_of(key):
        return pltpu.bitcast(key ^ ((key >> 31) & 0x7FFFFFFF), F32)

    def bisect(steps, count, value, lo, hi):
        def step(_, state):
            lo, hi, n_lo = state
            mid = lo + ((hi - lo) >> 1)
            n_mid = count(value(mid))
            ok = n_mid >= kf
            return jnp.where(ok, mid, lo), jnp.where(ok, hi, mid), jnp.where(ok, n_mid, n_lo)

        lo, _, n_lo = lax.fori_loop(0, steps, step, (lo, hi, jnp.full((1, TQ), kf, F32)))
        return lo, n_lo

    key16_inf = F32_INF_BITS >> 16
    coarse, _ = bisect(16, count_ge_rounded, lambda k16: val_of((k16 << 16) | ((k16 >> 31) & 0xFFFF)),
                       jnp.full((1, TQ), -key16_inf - 1, jnp.int32), jnp.full((1, TQ), key16_inf + 1, jnp.int32))
    key_inf = F32_INF_BITS
    key_coarse = (coarse << 16) | ((coarse >> 31) & 0xFFFF)
    fine_lo = jnp.maximum(key_coarse - (1 << 15), -key_inf - 1)
    fine_hi = jnp.minimum(key_coarse + (1 << 16), key_inf + 1)
    key_thr, n_ge = bisect(17, count_ge, val_of, fine_lo, fine_hi)
    thr = jnp.maximum(val_of(key_thr), -F32_MAX)
    has_ties = jnp.max(n_ge) > kf

    @pl.when(jnp.logical_not(has_ties))
    def _():
        def body(g, carry):
            for u in range(R):
                kt = g * R + u
                sc_ref[kt] = jnp.where(sc_ref[kt] >= thr, 0.0, neg_inf)
            return carry

        lax.fori_loop(0, ngroups, body, 0)

    @pl.when(has_ties)
    def _():
        def count_gt(g, acc):
            for u in range(R):
                acc = acc + fold(jnp.where(sc_ref[g * R + u] > thr, 1.0, 0.0), jnp.sum)
            return acc

        n_gt = lax.fori_loop(0, ngroups, count_gt, jnp.zeros((SUBLANES, TQ), F32))
        need = kf - jnp.sum(n_gt, axis=0, keepdims=True)
        incl = jnp.where(lax.broadcasted_iota(jnp.int32, (TK, TK), 1)
                         <= lax.broadcasted_iota(jnp.int32, (TK, TK), 0), 1.0, 0.0).astype(BF16)

        def body(g, seen):
            for u in range(R):
                kt = g * R + u
                sc = sc_ref[kt]
                eq = jnp.where(sc == thr, 1.0, 0.0)
                rank = seen + jnp.dot(incl, eq.astype(BF16), preferred_element_type=F32)
                keep = jnp.where(sc > thr, 1.0, jnp.where(rank <= need, eq, 0.0))
                sc_ref[kt] = jnp.where(keep > 0.0, 0.0, neg_inf)
                seen = seen + jnp.sum(eq, axis=0, keepdims=True)
            return seen

        lax.fori_loop(0, ngroups, body, jnp.zeros((1, TQ), F32))

    m_ref, l_ref = ml_ref.at[0], ml_ref.at[1]
    m_ref[...] = jnp.full(m_ref.shape, neg_inf, F32)
    l_ref[...] = jnp.zeros(l_ref.shape, F32)
    acc_ref[...] = jnp.zeros_like(acc_ref)

    def logits_tile(kt):
        kv = key_rows(ckv_ref, kt)
        near = jnp.clip(kt - (R * qi - 1), -1, DSA_NEAR - 1) + 1
        mask = sc_ref[kt]
        for h in range(H):
            lg = jnp.dot(kv, dqt_ref[h], preferred_element_type=F32) + (mask + bias_ref[h, near])
            lg_ref[h, kt] = lg
            m_ref[h] = jnp.maximum(m_ref[h], fold(lg, jnp.max))

    for_tiles(logits_tile)
    m_row = [jnp.max(m_ref[h], axis=0, keepdims=True) for h in range(H)]

    def pv_tile(kt):
        kv = key_rows(ckv_ref, kt)
        for h in range(H):
            p = jnp.exp2(lg_ref[h, kt] - m_row[h])
            l_ref[h] += fold(p, jnp.sum)
            acc_ref[h] += _tn_dot(kv, p.astype(BF16))

    for_tiles(pv_tile)
    for h in range(H):
        out_t = acc_ref[h] / jnp.sum(l_ref[h], axis=0, keepdims=True)
        o_ref[0, :, h * LANES:(h + 1) * LANES] = out_t.T.astype(o_ref.dtype)


def _dsa(z, rel_bias, batch, seq):
    TQ, TK = DSA_TQ, DSA_TK
    nq, nk = seq // TQ, seq // TK
    top_k = min(TOPK_MAX, seq // 4)
    H = DSA_HEADS

    def r3(a):
        return a.reshape(batch, seq, a.shape[-1])

    misc = r3(z["misc"])
    return pl.pallas_call(
        functools.partial(_dsa_kernel, top_k=top_k),
        grid=(batch, nq),
        in_specs=[pl.BlockSpec((1, TQ, IDX_HEADS * LANES), lambda b, q: (b, q, 0)),
                  pl.BlockSpec((1, TQ, LANES), lambda b, q: (b, q, 0)),
                  pl.BlockSpec((1, seq, LANES), lambda b, q: (b, 0, 0)),
                  pl.BlockSpec((1, seq, DSA_LATENT), lambda b, q: (b, 0, 0)),
                  pl.BlockSpec((1, TQ, H * DSA_LATENT), lambda b, q: (b, q, 0)),
                  _const_spec((H, DSA_NEAR + 1, TK, TQ))],
        out_specs=pl.BlockSpec((1, TQ, H * DSA_LATENT), lambda b, q: (b, q, 0)),
        out_shape=jax.ShapeDtypeStruct((batch, seq, H * DSA_LATENT), BF16),
        scratch_shapes=[pltpu.VMEM((nk, TK, TQ), F32),
                        pltpu.VMEM((nk, TK, TQ), BF16),
                        pltpu.VMEM((H, nk, TK, TQ), F32),
                        pltpu.VMEM((IDX_HEADS, LANES, TQ), BF16),
                        pltpu.VMEM((H, DSA_LATENT, TQ), BF16),
                        pltpu.VMEM((LANES, TQ), F32),
                        pltpu.VMEM((H, DSA_LATENT, TQ), F32),
                        pltpu.VMEM((2, H, SUBLANES, TQ), F32)],
        compiler_params=_params(("parallel", "arbitrary")),
        name="dsa",
    )(r3(z["iq"]), misc, misc, r3(z["ckv"]), r3(z["dq"]), _dsa_bias(rel_bias))


def _tail_kernel(h_ref, og_ref, od_ref, gt_ref, p_ref, wa_ref, wd_ref, wo_ref, g2_ref, win_ref, wout_ref,
                 gp_ref, wg_ref, wp_ref, gf_ref, o_ref, act_ref, *, ff):
    d = h_ref.shape[-1]
    ya = jnp.dot(og_ref[...], wa_ref[...], preferred_element_type=F32)
    yd = jnp.dot(od_ref[...], wd_ref[...], preferred_element_type=F32)
    gt = gt_ref[...].astype(F32)
    mix = jax.nn.sigmoid(gt[:, :d]) * ya + jax.nn.sigmoid(gt[:, d:]) * yd
    h = h_ref[...] + jnp.dot(mix.astype(BF16), wo_ref[...], preferred_element_type=F32)
    h = _half_swiglu(h, g2_ref, win_ref, wout_ref, act_ref, ff)
    hn = _rms(h, gp_ref[...]).astype(BF16)
    gate = jax.nn.sigmoid(jnp.dot(hn, wg_ref[...], preferred_element_type=F32))
    emb = jnp.dot(p_ref[...].astype(BF16), wp_ref[...], preferred_element_type=F32)
    o_ref[...] = _rms(h + gate * emb, gf_ref[...])


def _tail(h, og, od, gates, p, wa, wd, wo, g2, w_in, w_out, gp, wg, wp, gf, tm):
    m, d = h.shape
    ff = w_out.shape[0]
    row = lambda w: pl.BlockSpec((tm, w), lambda i: (i, 0))
    vec = lambda g: g.reshape(1, d)
    weights = [w.astype(BF16) for w in (wa, wd, wo)] + [vec(g2), w_in.astype(BF16), w_out.astype(BF16),
                                                         vec(gp), wg.astype(BF16), wp.astype(BF16), vec(gf)]
    return pl.pallas_call(
        functools.partial(_tail_kernel, ff=ff),
        grid=(m // tm,),
        in_specs=[row(d), row(og.shape[1]), row(od.shape[1]), row(2 * d), row(p.shape[1])]
                 + [_const_spec(w.shape) for w in weights],
        out_specs=row(d),
        out_shape=jax.ShapeDtypeStruct((m, d), F32),
        scratch_shapes=[pltpu.VMEM((tm, ff), BF16)],
        compiler_params=_params(("parallel",)),
        name="tail",
    )(h, og, od, gates, p, *weights)


def kernel(x, p, ffn1_norm, ffn1_w_in, ffn1_w_out, mix_norm, mix_w_in, gla_alpha_w, gla_alpha_b, gla_out_norm, gla_w_out, ckv_norm, dsa_w_out, rel_bias, mix_w_out, ffn2_norm, ffn2_w_in, ffn2_w_out, ple_norm, ple_w_gate, ple_w_proj, final_norm):
    batch, seq, d = x.shape
    assert p.shape[0] == 1, "single-layer trunk"
    m = batch * seq
    tm = min(TOKEN_TILE, m)
    h = x.reshape(m, d)
    h = _ffn(h, ffn1_norm[0], ffn1_w_in[0], ffn1_w_out[0], tm)
    z = _inproj(h, mix_norm[0], mix_w_in[0], ckv_norm[0], tm)
    og = _gla(z, gla_alpha_w[0], gla_alpha_b[0], gla_out_norm[0], batch, seq)
    od = _dsa(z, rel_bias, batch, seq)
    h = _tail(h, og.reshape(m, -1), od.reshape(m, -1), z["gates"], p[0].reshape(m, -1),
              gla_w_out[0], dsa_w_out[0], mix_w_out[0], ffn2_norm[0], ffn2_w_in[0], ffn2_w_out[0],
              ple_norm[0], ple_w_gate[0], ple_w_proj[0], final_norm, tm)
    return h.reshape(batch, seq, d)
```

```python
import functools
import math

import numpy as np
import jax
import jax.numpy as jnp
from jax import lax
from jax.experimental import pallas as pl
from jax.experimental.pallas import tpu as pltpu

F32 = jnp.float32
BF16 = jnp.bfloat16

EPS = 1e-6
GLA_HEADS = 4
GLA_DK = 128
GLA_DV = 256
GLA_LOWRANK = 16
GLA_TAU = 16.0
DSA_HEADS = 8
DSA_LATENT = 128
IDX_HEADS = 8
IDX_DIM = 64
TOPK_MAX = 256
REL_BUCKETS = 32
REL_MAX_DIST = 128

LANES = 128
SUBLANES = 8
VMEM_LIMIT = 56 * 1024 * 1024
LOG2_E = math.log2(math.e)
F32_INF_BITS = 0x7F800000
F32_MAX = float(np.finfo(np.float32).max)

TOKEN_TILE = 512
PROJ_COLS = 512
FFN_COLS = 512
GLA_CHUNK = 128
GLA_SUB = 16
DSA_TQ = 256
DSA_TK = 256

MISC_IK = 0
MISC_GA = IDX_DIM
MISC_IW = IDX_DIM + GLA_LOWRANK


def _nt_dot(a, b):
    return lax.dot_general(a, b, (((1,), (1,)), ((), ())), preferred_element_type=F32)


def _tn_dot(a, b):
    return lax.dot_general(a, b, (((0,), (0,)), ((), ())), preferred_element_type=F32)


def _rms(x, g):
    return x * lax.rsqrt(jnp.mean(x * x, axis=-1, keepdims=True) + EPS) * g


def _bf16_pieces(x):
    hi = x.astype(BF16)
    return hi, (x - hi.astype(F32)).astype(BF16)


def _dot_two_piece(a, b):
    a_hi, a_lo = _bf16_pieces(a)
    b_hi, b_lo = _bf16_pieces(b)
    dot = functools.partial(jnp.dot, preferred_element_type=F32)
    return dot(a_hi, b_hi) + dot(a_lo, b_hi) + dot(a_hi, b_lo)


def _const_spec(shape):
    nd = len(shape)
    return pl.BlockSpec(shape, lambda *_: (0,) * nd, pipeline_mode=pl.Buffered(1))


def _params(sem):
    return pltpu.CompilerParams(dimension_semantics=sem, vmem_limit_bytes=VMEM_LIMIT)


def _half_swiglu(x, g_ref, win_ref, wout_ref, act_ref, ff):
    xn = _rms(x, g_ref[...]).astype(BF16)
    for c0 in range(0, ff, FFN_COLS):
        cw = min(FFN_COLS, ff - c0)
        gate = jnp.dot(xn, win_ref[:, c0:c0 + cw], preferred_element_type=F32)
        up = jnp.dot(xn, win_ref[:, ff + c0:ff + c0 + cw], preferred_element_type=F32)
        act_ref[:, c0:c0 + cw] = (gate * jax.nn.sigmoid(gate) * up).astype(BF16)
    return x + 0.5 * jnp.dot(act_ref[...], wout_ref[...], preferred_element_type=F32)


def _ffn_kernel(h_ref, g_ref, win_ref, wout_ref, o_ref, act_ref, *, ff):
    o_ref[...] = _half_swiglu(h_ref[...], g_ref, win_ref, wout_ref, act_ref, ff)


def _ffn(h, g, w_in, w_out, tm):
    m, d = h.shape
    ff = w_out.shape[0]
    assert ff % LANES == 0
    return pl.pallas_call(
        functools.partial(_ffn_kernel, ff=ff),
        grid=(m // tm,),
        in_specs=[pl.BlockSpec((tm, d), lambda i: (i, 0)),
                  _const_spec((1, d)),
                  _const_spec((d, 2 * ff)),
                  _const_spec((ff, d))],
        out_specs=pl.BlockSpec((tm, d), lambda i: (i, 0)),
        out_shape=jax.ShapeDtypeStruct((m, d), F32),
        scratch_shapes=[pltpu.VMEM((tm, ff), BF16)],
        compiler_params=_params(("parallel",)),
        name="ffn",
    )(h, g.reshape(1, d), w_in.astype(BF16), w_out.astype(BF16))


_PROJ_OUT = (("gq", GLA_HEADS * GLA_DK, F32), ("gk", GLA_HEADS * GLA_DK, F32),
             ("gv", GLA_HEADS * GLA_DV, BF16), ("gr", GLA_HEADS * GLA_DV, BF16),
             ("dq", DSA_HEADS * DSA_LATENT, BF16), ("iq", IDX_HEADS * LANES, BF16),
             ("gates", None, BF16), ("ckv", DSA_LATENT, BF16), ("misc", LANES, F32))
_PROJ_TRANSPOSED = ("dq", "iq")


def _inproj_kernel(h_ref, g_ref, ckvg_ref, wa_ref, wb_ref, *out_refs, widths):
    xn = _rms(h_ref[...], g_ref[...]).astype(BF16)
    w_ref, col = wa_ref, 0
    for (name, _, dt), width, o_ref in zip(_PROJ_OUT, widths, out_refs):
        if col == wa_ref.shape[1] and w_ref is wa_ref:
            w_ref, col = wb_ref, 0
        for c0 in range(0, width, PROJ_COLS):
            cw = min(PROJ_COLS, width - c0)
            z = jnp.dot(xn, w_ref[:, col + c0:col + c0 + cw], preferred_element_type=F32)
            if name == "ckv":
                z = _rms(z, ckvg_ref[...])
            elif name == "dq":
                z = z * (DSA_LATENT ** -0.5 * LOG2_E)
            if name in _PROJ_TRANSPOSED:
                zt = z.T
                for j in range(z.shape[0] // DSA_TQ):
                    o_ref[j, c0:c0 + cw, :] = zt[:, j * DSA_TQ:(j + 1) * DSA_TQ].astype(dt)
            else:
                o_ref[:, c0:c0 + cw] = z.astype(dt)
        col += width


def _inproj(h, g, w_in, ckv_g, tm):
    m, d = h.shape
    sizes = (GLA_HEADS * GLA_DK, GLA_HEADS * GLA_DK, GLA_HEADS * GLA_DV, GLA_HEADS * GLA_DV, GLA_LOWRANK,
             DSA_HEADS * DSA_LATENT, DSA_LATENT, IDX_HEADS * IDX_DIM, IDX_DIM, IDX_HEADS, 2 * d)
    pts = np.cumsum(sizes)[:-1].tolist()
    gq, gk, gv, gr, ga, dq, dkv, iq, ik, iw, gates = jnp.split(w_in, pts, axis=-1)
    iq = jnp.pad(iq.reshape(d, IDX_HEADS, IDX_DIM), ((0, 0), (0, 0), (0, LANES - IDX_DIM))).reshape(d, -1)
    misc = jnp.pad(jnp.concatenate([ik, ga, iw], axis=-1),
                   ((0, 0), (0, LANES - IDX_DIM - GLA_LOWRANK - IDX_HEADS)))
    wa = w_in[:, :pts[3]].astype(BF16)
    wb = jnp.concatenate([dq, iq, gates, dkv, misc], axis=-1).astype(BF16)
    widths = tuple(wd if wd is not None else 2 * d for _, wd, _ in _PROJ_OUT)
    assert wa.shape[1] == sum(widths[:4]) and wa.shape[1] + wb.shape[1] == sum(widths)
    outs = pl.pallas_call(
        functools.partial(_inproj_kernel, widths=widths),
        grid=(m // tm,),
        in_specs=[pl.BlockSpec((tm, d), lambda i: (i, 0)),
                  _const_spec((1, d)),
                  _const_spec((1, DSA_LATENT)),
                  _const_spec(wa.shape),
                  _const_spec(wb.shape)],
        out_specs=[pl.BlockSpec((tm // DSA_TQ, wd, DSA_TQ), lambda i: (i, 0, 0)) if name in _PROJ_TRANSPOSED
                   else pl.BlockSpec((tm, wd), lambda i: (i, 0)) for wd, (name, _, _) in zip(widths, _PROJ_OUT)],
        out_shape=[jax.ShapeDtypeStruct((m // DSA_TQ, wd, DSA_TQ) if name in _PROJ_TRANSPOSED else (m, wd), dt)
                   for wd, (name, _, dt) in zip(widths, _PROJ_OUT)],
        compiler_params=_params(("parallel",)),
        name="inproj",
    )(h, g.reshape(1, d), ckv_g.reshape(1, DSA_LATENT), wa, wb)
    return dict(zip([n for n, _, _ in _PROJ_OUT], outs))


def _gla_scores_safe(qs, k, b, a_ref):
    C, SUB = GLA_CHUNK, GLA_SUB
    sub_t = lax.broadcasted_iota(jnp.int32, (SUB, 1), 0)
    out_lane = lax.broadcasted_iota(jnp.int32, (SUB, C), 1)
    for i in range(C // SUB):
        r0 = i * SUB
        b_i = b[r0:r0 + SUB]
        q_i = qs[r0:r0 + SUB]
        if i == 0:
            a_row = jnp.zeros((SUB, C), F32)
        else:
            beta = b[r0 - 1:r0]
            q_t = (q_i * jnp.exp(b_i - beta)).astype(BF16)
            k_t = (k[:r0] * jnp.exp(beta - b[:r0])).astype(BF16)
            k_t = jnp.concatenate([k_t, jnp.zeros((C - r0, GLA_DK), BF16)], axis=0)
            a_row = _nt_dot(q_t, k_t)
        for s in range(SUB):
            e = jnp.exp(jnp.where(sub_t >= s, b_i - b[r0 + s:r0 + s + 1], -jnp.inf))
            col = jnp.sum(q_i * k[r0 + s:r0 + s + 1] * e, axis=-1, keepdims=True)
            a_row = jnp.where(out_lane == r0 + s, col, a_row)
        a_ref[r0:r0 + SUB, :] = a_row


def _gla_kernel(q_ref, k_ref, v_ref, gr_ref, misc_ref, aw_ref, ab_ref, ng_ref, o_ref, st_ref, a_ref):
    C = GLA_CHUNK
    DK, DV = GLA_DK, GLA_DV

    @pl.when(pl.program_id(1) == 0)
    def _():
        st_ref[...] = jnp.zeros_like(st_ref)

    x = _dot_two_piece(misc_ref[0], aw_ref[...]) + ab_ref[...]
    la = (jnp.minimum(x, 0.0) - jnp.log1p(jnp.exp(-jnp.abs(x)))) / GLA_TAU
    row = lax.broadcasted_iota(jnp.int32, (C, C), 0)
    col = lax.broadcasted_iota(jnp.int32, (C, C), 1)
    causal = col <= row
    ones_tri = jnp.where(causal, 1.0, 0.0).astype(BF16)
    la_hi, la_lo = _bf16_pieces(la)
    b_all = (jnp.dot(ones_tri, la_hi, preferred_element_type=F32)
             + jnp.dot(ones_tri, la_lo, preferred_element_type=F32))

    def head(h):
        b = b_all[:, h * DK:(h + 1) * DK]
        k = k_ref[0, :, h * DK:(h + 1) * DK]
        qs = q_ref[0, :, h * DK:(h + 1) * DK] * (DK ** -0.5)
        return b, k, qs, qs * jnp.exp(b)

    bound = jnp.zeros((1, DK), F32)
    for h in range(GLA_HEADS):
        b, k, _, q_in = head(h)
        k_max = jnp.max(jnp.abs(k), axis=0, keepdims=True) * jnp.exp(-b[C - 1:C])
        bound = bound + jnp.maximum(jnp.max(jnp.abs(q_in), axis=0, keepdims=True), 1.0) * k_max
    factored_ok = jnp.sum(bound) < 1e30

    def finish(h, b, k, q_in, scores):
        v = v_ref[0, :, h * DV:(h + 1) * DV]
        st = st_ref[h]
        o = _nt_dot(q_in.astype(BF16), st.astype(BF16))
        o = o + jnp.dot(scores.astype(BF16), v, preferred_element_type=F32)
        o = _rms(o, ng_ref[:, h * DV:(h + 1) * DV])
        gr = gr_ref[0, :, h * DV:(h + 1) * DV].astype(F32)
        o_ref[0, :, h * DV:(h + 1) * DV] = (o * (gr * jax.nn.sigmoid(gr))).astype(o_ref.dtype)
        b_last = b[C - 1:C]
        k_dec = (k * jnp.exp(b_last - b)).astype(BF16)
        st_ref[h] = st * jnp.exp(b_last) + _tn_dot(v, k_dec)

    @pl.when(factored_ok)
    def _():
        for h in range(GLA_HEADS):
            b, k, _, q_in = head(h)
            k_out = k * jnp.exp(-b)
            scores = jnp.where(causal, _nt_dot(q_in.astype(BF16), k_out.astype(BF16)), 0.0)
            finish(h, b, k, q_in, scores)

    @pl.when(jnp.logical_not(factored_ok))
    def _():
        for h in range(GLA_HEADS):
            b, k, qs, q_in = head(h)
            _gla_scores_safe(qs, k, b, a_ref)
            finish(h, b, k, q_in, a_ref[...])


def _gla(z, alpha_w, alpha_b, norm_g, batch, seq):
    C = GLA_CHUNK
    H, DK, DV = GLA_HEADS, GLA_DK, GLA_DV

    def r3(a):
        return a.reshape(batch, seq, a.shape[-1])

    aw = jnp.zeros((LANES, H * DK), F32).at[MISC_GA:MISC_GA + GLA_LOWRANK, :].set(alpha_w)
    tok = lambda w: pl.BlockSpec((1, C, w), lambda b, c: (b, c, 0))
    return pl.pallas_call(
        _gla_kernel,
        grid=(batch, seq // C),
        in_specs=[tok(H * DK), tok(H * DK), tok(H * DV), tok(H * DV), tok(LANES),
                  _const_spec((LANES, H * DK)), _const_spec((1, H * DK)), _const_spec((1, H * DV))],
        out_specs=tok(H * DV),
        out_shape=jax.ShapeDtypeStruct((batch, seq, H * DV), BF16),
        scratch_shapes=[pltpu.VMEM((H, DV, DK), F32), pltpu.VMEM((C, C), F32)],
        compiler_params=_params(("parallel", "arbitrary")),
        name="gla",
    )(r3(z["gq"]), r3(z["gk"]), r3(z["gv"]), r3(z["gr"]), r3(z["misc"]),
      aw, alpha_b.reshape(1, H * DK), norm_g.reshape(1, H * DV))


DSA_NEAR = DSA_TQ // DSA_TK + 1


def _t5_bucket(dist):
    max_exact = REL_BUCKETS // 2
    d = np.maximum(dist, 0)
    steps = (np.log(np.maximum(d, 1).astype(np.float64) / max_exact) / math.log(REL_MAX_DIST / max_exact)
             * (REL_BUCKETS - max_exact))
    live = (d > max_exact) & (steps < REL_BUCKETS - max_exact - 0.5)
    assert np.all(np.abs(steps - np.round(steps))[live] > 1e-4)
    large = np.minimum(max_exact + steps.astype(np.int32), REL_BUCKETS - 1)
    return np.where(d < max_exact, d, large).astype(np.int32)


def _dsa_bias_kernel(rb_ref, bkt_ref, o_ref):
    h = pl.program_id(0)
    bkt = bkt_ref[...]
    far = rb_ref[REL_BUCKETS - 1, h]
    acc = jnp.zeros(bkt.shape, F32)
    for j in range(REL_BUCKETS):
        acc = jnp.where(bkt == j, rb_ref[j, h] - far, acc)
    o_ref[0] = acc * LOG2_E


def _dsa_bias(rel_bias):
    assert DSA_TK + 1 >= REL_MAX_DIST
    s = np.arange(DSA_TK, dtype=np.int32)[:, None]
    t = np.arange(DSA_TQ, dtype=np.int32)[None, :]
    near = [_t5_bucket(t - s + DSA_TK * (1 - j)) for j in range(DSA_NEAR)]
    bkt = jnp.asarray(np.stack([np.full((DSA_TK, DSA_TQ), -1, np.int32)] + near))
    n = DSA_NEAR + 1
    return pl.pallas_call(
        _dsa_bias_kernel,
        grid=(DSA_HEADS,),
        in_specs=[pl.BlockSpec(memory_space=pltpu.SMEM),
                  pl.BlockSpec((n, DSA_TK, DSA_TQ), lambda h: (0, 0, 0))],
        out_specs=pl.BlockSpec((1, n, DSA_TK, DSA_TQ), lambda h: (h, 0, 0, 0)),
        out_shape=jax.ShapeDtypeStruct((DSA_HEADS, n, DSA_TK, DSA_TQ), F32),
        name="dsa_bias",
    )(rel_bias, bkt)


def _dsa_kernel(iqt_ref, mq_ref, mk_ref, ckv_ref, dqt_ref, bias_ref, o_ref,
                sc_ref, scb_ref, lg_ref, wt_ref, acc_ref, ml_ref, *, top_k):
    TQ, TK = DSA_TQ, DSA_TK
    R = TQ // TK
    H = DSA_HEADS
    qi = pl.program_id(1)
    ngroups = qi + 1
    neg_inf = -jnp.inf
    kf = float(top_k)

    wt_ref[...] = mq_ref[0].T * ((IDX_HEADS ** -0.5) * (IDX_DIM ** -0.5))

    def head_rows(ref, h):
        return ref[0, h * LANES:(h + 1) * LANES, :]

    def key_rows(ref, kt):
        return ref[0, pl.ds(pl.multiple_of(kt * TK, TK), TK), :]

    def fold(x, op):
        return op(x.reshape(TK // SUBLANES, SUBLANES, TQ), axis=0)

    k_pos = lax.broadcasted_iota(jnp.int32, (TK, TQ), 0)
    q_pos = lax.broadcasted_iota(jnp.int32, (TK, TQ), 1) + qi * TQ

    def for_tiles(tile_fn):
        def pair(g, carry):
            tile_fn(2 * g)
            tile_fn(2 * g + 1)
            return carry

        lax.fori_loop(0, ngroups // 2, pair, 0)

        @pl.when(ngroups % 2 == 1)
        def _():
            tile_fn(ngroups - 1)

    def score_tile(kt):
        keys = key_rows(mk_ref, kt).astype(BF16)
        acc = jnp.zeros((TK, TQ), F32)
        for h in range(IDX_HEADS):
            d = jnp.dot(keys, head_rows(iqt_ref, h), preferred_element_type=F32)
            acc = acc + jnp.maximum(d, 0.0) * wt_ref[MISC_IW + h:MISC_IW + h + 1, :]
        sc = jnp.where(k_pos + kt * TK <= q_pos, acc, neg_inf)
        sc_ref[kt] = sc
        scb_ref[kt] = sc.astype(BF16)

    for_tiles(score_tile)

    rows = 4 * SUBLANES

    def count_tiles(hits, dtype):
        def pair(g, acc):
            return acc + hits(2 * g) + hits(2 * g + 1)

        acc = lax.fori_loop(0, ngroups // 2, pair, jnp.zeros((rows, TQ), dtype))
        acc = lax.cond(ngroups % 2 == 1, lambda a: a + hits(ngroups - 1), lambda a: a, acc)
        return jnp.sum(acc.astype(F32), axis=0, keepdims=True)

    def count_ge(cand):
        def hits(kt):
            hit = jnp.where(sc_ref[kt] >= cand, 1.0, 0.0)
            return jnp.sum(hit.reshape(TK // rows, rows, TQ), axis=0)

        return count_tiles(hits, F32)

    def count_ge_rounded(cand):
        cand = cand.astype(BF16)

        def hits(kt):
            hit = jnp.where(scb_ref[kt] >= cand, jnp.ones((), BF16), jnp.zeros((), BF16))
            acc = hit[:rows]
            for i in range(1, TK // rows):
                acc = acc + hit[i * rows:(i + 1) * rows]
            return acc

        return count_tiles(hits, BF16)

    def val_of(key):
        return pltpu.bitcast(key ^ ((key >> 31) & 0x7FFFFFFF), F32)

    def bisect(steps, count, value, lo, hi):
        def step(_, state):
            lo, hi, n_lo = state
            mid = lo + ((hi - lo) >> 1)
            n_mid = count(value(mid))
            ok = n_mid >= kf
            return jnp.where(ok, mid, lo), jnp.where(ok, hi, mid), jnp.where(ok, n_mid, n_lo)

        lo, _, n_lo = lax.fori_loop(0, steps, step, (lo, hi, jnp.full((1, TQ), kf, F32)))
        return lo, n_lo

    key16_inf = F32_INF_BITS >> 16
    coarse, _ = bisect(16, count_ge_rounded, lambda k16: val_of((k16 << 16) | ((k16 >> 31) & 0xFFFF)),
                       jnp.full((1, TQ), -key16_inf - 1, jnp.int32), jnp.full((1, TQ), key16_inf + 1, jnp.int32))
    key_inf = F32_INF_BITS
    key_coarse = (coarse << 16) | ((coarse >> 31) & 0xFFFF)
    fine_lo = jnp.maximum(key_coarse - (1 << 15), -key_inf - 1)
    fine_hi = jnp.minimum(key_coarse + (1 << 16), key_inf + 1)
    key_thr, n_ge = bisect(17, count_ge, val_of, fine_lo, fine_hi)
    thr = jnp.maximum(val_of(key_thr), -F32_MAX)
    has_ties = jnp.max(n_ge) > kf

    @pl.when(jnp.logical_not(has_ties))
    def _():
        def body(g, carry):
            for u in range(R):
                kt = g * R + u
                sc_ref[kt] = jnp.where(sc_ref[kt] >= thr, 0.0, neg_inf)
            return carry

        lax.fori_loop(0, ngroups, body, 0)

    @pl.when(has_ties)
    def _():
        def count_gt(g, acc):
            for u in range(R):
                acc = acc + fold(jnp.where(sc_ref[g * R + u] > thr, 1.0, 0.0), jnp.sum)
            return acc

        n_gt = lax.fori_loop(0, ngroups, count_gt, jnp.zeros((SUBLANES, TQ), F32))
        need = kf - jnp.sum(n_gt, axis=0, keepdims=True)
        incl = jnp.where(lax.broadcasted_iota(jnp.int32, (TK, TK), 1)
                         <= lax.broadcasted_iota(jnp.int32, (TK, TK), 0), 1.0, 0.0).astype(BF16)

        def body(g, seen):
            for u in range(R):
                kt = g * R + u
                sc = sc_ref[kt]
                eq = jnp.where(sc == thr, 1.0, 0.0)
                rank = seen + jnp.dot(incl, eq.astype(BF16), preferred_element_type=F32)
                keep = jnp.where(sc > thr, 1.0, jnp.where(rank <= need, eq, 0.0))
                sc_ref[kt] = jnp.where(keep > 0.0, 0.0, neg_inf)
                seen = seen + jnp.sum(eq, axis=0, keepdims=True)
            return seen

        lax.fori_loop(0, ngroups, body, jnp.zeros((1, TQ), F32))

    m_ref, l_ref = ml_ref.at[0], ml_ref.at[1]
    m_ref[...] = jnp.full(m_ref.shape, neg_inf, F32)
    l_ref[...] = jnp.zeros(l_ref.shape, F32)
    acc_ref[...] = jnp.zeros_like(acc_ref)

    def logits_tile(kt):
        kv = key_rows(ckv_ref, kt)
        near = jnp.clip(kt - (R * qi - 1), -1, DSA_NEAR - 1) + 1
        mask = sc_ref[kt]
        for h in range(H):
            lg = jnp.dot(kv, head_rows(dqt_ref, h), preferred_element_type=F32) + (mask + bias_ref[h, near])
            lg_ref[h, kt] = lg
            m_ref[h] = jnp.maximum(m_ref[h], fold(lg, jnp.max))

    for_tiles(logits_tile)
    m_row = [jnp.max(m_ref[h], axis=0, keepdims=True) for h in range(H)]

    def pv_tile(kt):
        kv = key_rows(ckv_ref, kt)
        for h in range(H):
            p = jnp.exp2(lg_ref[h, kt] - m_row[h])
            l_ref[h] += fold(p, jnp.sum)
            acc_ref[h] += _tn_dot(kv, p.astype(BF16))

    for_tiles(pv_tile)
    for h in range(H):
        out_t = acc_ref[h] / jnp.sum(l_ref[h], axis=0, keepdims=True)
        o_ref[0, :, h * LANES:(h + 1) * LANES] = out_t.T.astype(o_ref.dtype)


def _dsa(z, rel_bias, batch, seq):
    TQ, TK = DSA_TQ, DSA_TK
    nq, nk = seq // TQ, seq // TK
    top_k = min(TOPK_MAX, seq // 4)
    H = DSA_HEADS

    def r3(a):
        return a.reshape(batch, seq, a.shape[-1])

    misc = r3(z["misc"])
    return pl.pallas_call(
        functools.partial(_dsa_kernel, top_k=top_k),
        grid=(batch, nq),
        in_specs=[pl.BlockSpec((1, IDX_HEADS * LANES, TQ), lambda b, q: (b * nq + q, 0, 0)),
                  pl.BlockSpec((1, TQ, LANES), lambda b, q: (b, q, 0)),
                  pl.BlockSpec((1, seq, LANES), lambda b, q: (b, 0, 0)),
                  pl.BlockSpec((1, seq, DSA_LATENT), lambda b, q: (b, 0, 0)),
                  pl.BlockSpec((1, H * DSA_LATENT, TQ), lambda b, q: (b * nq + q, 0, 0)),
                  _const_spec((H, DSA_NEAR + 1, TK, TQ))],
        out_specs=pl.BlockSpec((1, TQ, H * DSA_LATENT), lambda b, q: (b, q, 0)),
        out_shape=jax.ShapeDtypeStruct((batch, seq, H * DSA_LATENT), BF16),
        scratch_shapes=[pltpu.VMEM((nk, TK, TQ), F32),
                        pltpu.VMEM((nk, TK, TQ), BF16),
                        pltpu.VMEM((H, nk, TK, TQ), F32),
                        pltpu.VMEM((LANES, TQ), F32),
                        pltpu.VMEM((H, DSA_LATENT, TQ), F32),
                        pltpu.VMEM((2, H, SUBLANES, TQ), F32)],
        compiler_params=_params(("parallel", "arbitrary")),
        name="dsa",
    )(z["iq"], misc, misc, r3(z["ckv"]), z["dq"], _dsa_bias(rel_bias))


def _tail_kernel(h_ref, og_ref, od_ref, gt_ref, p_ref, wa_ref, wd_ref, wo_ref, g2_ref, win_ref, wout_ref,
                 gp_ref, wg_ref, wp_ref, gf_ref, o_ref, act_ref, *, ff):
    d = h_ref.shape[-1]
    ya = jnp.dot(og_ref[...], wa_ref[...], preferred_element_type=F32)
    yd = jnp.dot(od_ref[...], wd_ref[...], preferred_element_type=F32)
    gt = gt_ref[...].astype(F32)
    mix = jax.nn.sigmoid(gt[:, :d]) * ya + jax.nn.sigmoid(gt[:, d:]) * yd
    h = h_ref[...] + jnp.dot(mix.astype(BF16), wo_ref[...], preferred_element_type=F32)
    h = _half_swiglu(h, g2_ref, win_ref, wout_ref, act_ref, ff)
    hn = _rms(h, gp_ref[...]).astype(BF16)
    gate = jax.nn.sigmoid(jnp.dot(hn, wg_ref[...], preferred_element_type=F32))
    emb = jnp.dot(p_ref[...].astype(BF16), wp_ref[...], preferred_element_type=F32)
    o_ref[...] = _rms(h + gate * emb, gf_ref[...])


def _tail(h, og, od, gates, p, wa, wd, wo, g2, w_in, w_out, gp, wg, wp, gf, tm):
    m, d = h.shape
    ff = w_out.shape[0]
    row = lambda w: pl.BlockSpec((tm, w), lambda i: (i, 0))
    vec = lambda g: g.reshape(1, d)
    weights = [w.astype(BF16) for w in (wa, wd, wo)] + [vec(g2), w_in.astype(BF16), w_out.astype(BF16),
                                                         vec(gp), wg.astype(BF16), wp.astype(BF16), vec(gf)]
    return pl.pallas_call(
        functools.partial(_tail_kernel, ff=ff),
        grid=(m // tm,),
        in_specs=[row(d), row(og.shape[1]), row(od.shape[1]), row(2 * d), row(p.shape[1])]
                 + [_const_spec(w.shape) for w in weights],
        out_specs=row(d),
        out_shape=jax.ShapeDtypeStruct((m, d), F32),
        scratch_shapes=[pltpu.VMEM((tm, ff), BF16)],
        compiler_params=_params(("parallel",)),
        name="tail",
    )(h, og, od, gates, p, *weights)


def kernel(x, p, ffn1_norm, ffn1_w_in, ffn1_w_out, mix_norm, mix_w_in, gla_alpha_w, gla_alpha_b, gla_out_norm, gla_w_out, ckv_norm, dsa_w_out, rel_bias, mix_w_out, ffn2_norm, ffn2_w_in, ffn2_w_out, ple_norm, ple_w_gate, ple_w_proj, final_norm):
    batch, seq, d = x.shape
    assert p.shape[0] == 1, "single-layer trunk"
    m = batch * seq
    tm = min(TOKEN_TILE, m)
    h = x.reshape(m, d)
    h = _ffn(h, ffn1_norm[0], ffn1_w_in[0], ffn1_w_out[0], tm)
    z = _inproj(h, mix_norm[0], mix_w_in[0], ckv_norm[0], tm)
    og = _gla(z, gla_alpha_w[0], gla_alpha_b[0], gla_out_norm[0], batch, seq)
    od = _dsa(z, rel_bias, batch, seq)
    h = _tail(h, og.reshape(m, -1), od.reshape(m, -1), z["gates"], p[0].reshape(m, -1),
              gla_w_out[0], dsa_w_out[0], mix_w_out[0], ffn2_norm[0], ffn2_w_in[0], ffn2_w_out[0],
              ple_norm[0], ple_w_gate[0], ple_w_proj[0], final_norm, tm)
    return h.reshape(batch, seq, d)
```

```python
import functools
import math

import numpy as np
import jax
import jax.numpy as jnp
from jax import lax
from jax.experimental import pallas as pl
from jax.experimental.pallas import tpu as pltpu

F32 = jnp.float32
BF16 = jnp.bfloat16

EPS = 1e-6
GLA_HEADS = 4
GLA_DK = 128
GLA_DV = 256
GLA_LOWRANK = 16
GLA_TAU = 16.0
DSA_HEADS = 8
DSA_LATENT = 128
IDX_HEADS = 8
IDX_DIM = 64
TOPK_MAX = 256
REL_BUCKETS = 32
REL_MAX_DIST = 128

LANES = 128
SUBLANES = 8
VMEM_LIMIT = 56 * 1024 * 1024
LOG2_E = math.log2(math.e)
F32_INF_BITS = 0x7F800000
F32_MAX = float(np.finfo(np.float32).max)

TOKEN_TILE = 512
PROJ_COLS = 512
FFN_COLS = 512
GLA_CHUNK = 128
GLA_SUB = 16
DSA_TQ = 256
DSA_TK = 256

MISC_IK = 0
MISC_GA = IDX_DIM
MISC_IW = IDX_DIM + GLA_LOWRANK


def _nt_dot(a, b):
    return lax.dot_general(a, b, (((1,), (1,)), ((), ())), preferred_element_type=F32)


def _tn_dot(a, b):
    return lax.dot_general(a, b, (((0,), (0,)), ((), ())), preferred_element_type=F32)


def _rms(x, g):
    return x * lax.rsqrt(jnp.mean(x * x, axis=-1, keepdims=True) + EPS) * g


def _bf16_pieces(x):
    hi = x.astype(BF16)
    return hi, (x - hi.astype(F32)).astype(BF16)


def _dot_two_piece(a, b):
    a_hi, a_lo = _bf16_pieces(a)
    b_hi, b_lo = _bf16_pieces(b)
    dot = functools.partial(jnp.dot, preferred_element_type=F32)
    return dot(a_hi, b_hi) + dot(a_lo, b_hi) + dot(a_hi, b_lo)


def _const_spec(shape):
    nd = len(shape)
    return pl.BlockSpec(shape, lambda *_: (0,) * nd, pipeline_mode=pl.Buffered(1))


def _params(sem):
    return pltpu.CompilerParams(dimension_semantics=sem, vmem_limit_bytes=VMEM_LIMIT)


def _half_swiglu(x, g_ref, win_ref, wout_ref, act_ref, ff):
    xn = _rms(x, g_ref[...]).astype(BF16)
    for c0 in range(0, ff, FFN_COLS):
        cw = min(FFN_COLS, ff - c0)
        gate = jnp.dot(xn, win_ref[:, c0:c0 + cw], preferred_element_type=F32)
        up = jnp.dot(xn, win_ref[:, ff + c0:ff + c0 + cw], preferred_element_type=F32)
        act_ref[:, c0:c0 + cw] = (gate * jax.nn.sigmoid(gate) * up).astype(BF16)
    return x + 0.5 * jnp.dot(act_ref[...], wout_ref[...], preferred_element_type=F32)


def _ffn_kernel(h_ref, g_ref, win_ref, wout_ref, o_ref, act_ref, *, ff):
    o_ref[...] = _half_swiglu(h_ref[...], g_ref, win_ref, wout_ref, act_ref, ff)


def _ffn(h, g, w_in, w_out, tm):
    m, d = h.shape
    ff = w_out.shape[0]
    assert ff % LANES == 0
    return pl.pallas_call(
        functools.partial(_ffn_kernel, ff=ff),
        grid=(m // tm,),
        in_specs=[pl.BlockSpec((tm, d), lambda i: (i, 0)),
                  _const_spec((1, d)),
                  _const_spec((d, 2 * ff)),
                  _const_spec((ff, d))],
        out_specs=pl.BlockSpec((tm, d), lambda i: (i, 0)),
        out_shape=jax.ShapeDtypeStruct((m, d), F32),
        scratch_shapes=[pltpu.VMEM((tm, ff), BF16)],
        compiler_params=_params(("parallel",)),
        name="ffn",
    )(h, g.reshape(1, d), w_in.astype(BF16), w_out.astype(BF16))


_PROJ_OUT = (("gq", GLA_HEADS * GLA_DK, F32), ("gk", GLA_HEADS * GLA_DK, F32),
             ("gv", GLA_HEADS * GLA_DV, BF16), ("gr", GLA_HEADS * GLA_DV, BF16),
             ("dq", DSA_HEADS * DSA_LATENT, BF16), ("iq", IDX_HEADS * IDX_DIM, BF16),
             ("gates", None, BF16), ("ckv", DSA_LATENT, BF16), ("misc", LANES, F32))
_PROJ_TRANSPOSED = ("dq", "iq")


def _inproj_kernel(h_ref, g_ref, ckvg_ref, wa_ref, wb_ref, *out_refs, widths):
    xn = _rms(h_ref[...], g_ref[...]).astype(BF16)
    w_ref, col = wa_ref, 0
    for (name, _, dt), width, o_ref in zip(_PROJ_OUT, widths, out_refs):
        if col == wa_ref.shape[1] and w_ref is wa_ref:
            w_ref, col = wb_ref, 0
        for c0 in range(0, width, PROJ_COLS):
            cw = min(PROJ_COLS, width - c0)
            z = jnp.dot(xn, w_ref[:, col + c0:col + c0 + cw], preferred_element_type=F32)
            if name == "ckv":
                z = _rms(z, ckvg_ref[...])
            elif name == "dq":
                z = z * (DSA_LATENT ** -0.5 * LOG2_E)
            if name in _PROJ_TRANSPOSED:
                zt = z.T
                for j in range(z.shape[0] // DSA_TQ):
                    block = zt[:, j * DSA_TQ:(j + 1) * DSA_TQ].astype(dt)
                    if name == "iq":
                        assert c0 == 0 and cw == width
                        for h in range(IDX_HEADS):
                            o_ref[j, h * LANES:h * LANES + IDX_DIM, :] = block[h * IDX_DIM:(h + 1) * IDX_DIM]
                            o_ref[j, h * LANES + IDX_DIM:(h + 1) * LANES, :] = jnp.zeros(
                                (LANES - IDX_DIM, DSA_TQ), dt)
                    else:
                        o_ref[j, c0:c0 + cw, :] = block
            else:
                o_ref[:, c0:c0 + cw] = z.astype(dt)
        col += width


def _inproj(h, g, w_in, ckv_g, tm):
    m, d = h.shape
    sizes = (GLA_HEADS * GLA_DK, GLA_HEADS * GLA_DK, GLA_HEADS * GLA_DV, GLA_HEADS * GLA_DV, GLA_LOWRANK,
             DSA_HEADS * DSA_LATENT, DSA_LATENT, IDX_HEADS * IDX_DIM, IDX_DIM, IDX_HEADS, 2 * d)
    pts = np.cumsum(sizes)[:-1].tolist()
    gq, gk, gv, gr, ga, dq, dkv, iq, ik, iw, gates = jnp.split(w_in, pts, axis=-1)
    misc = jnp.pad(jnp.concatenate([ik, ga, iw], axis=-1),
                   ((0, 0), (0, LANES - IDX_DIM - GLA_LOWRANK - IDX_HEADS)))
    wa = w_in[:, :pts[3]].astype(BF16)
    wb = jnp.concatenate([dq, iq, gates, dkv, misc], axis=-1).astype(BF16)
    widths = tuple(wd if wd is not None else 2 * d for _, wd, _ in _PROJ_OUT)
    assert wa.shape[1] == sum(widths[:4]) and wa.shape[1] + wb.shape[1] == sum(widths)
    out_widths = tuple(IDX_HEADS * LANES if name == "iq" else wd for wd, (name, _, _) in zip(widths, _PROJ_OUT))
    outs = pl.pallas_call(
        functools.partial(_inproj_kernel, widths=widths),
        grid=(m // tm,),
        in_specs=[pl.BlockSpec((tm, d), lambda i: (i, 0)),
                  _const_spec((1, d)),
                  _const_spec((1, DSA_LATENT)),
                  _const_spec(wa.shape),
                  _const_spec(wb.shape)],
        out_specs=[pl.BlockSpec((tm // DSA_TQ, wd, DSA_TQ), lambda i: (i, 0, 0)) if name in _PROJ_TRANSPOSED
                   else pl.BlockSpec((tm, wd), lambda i: (i, 0)) for wd, (name, _, _) in zip(out_widths, _PROJ_OUT)],
        out_shape=[jax.ShapeDtypeStruct((m // DSA_TQ, wd, DSA_TQ) if name in _PROJ_TRANSPOSED else (m, wd), dt)
                   for wd, (name, _, dt) in zip(out_widths, _PROJ_OUT)],
        compiler_params=_params(("parallel",)),
        name="inproj",
    )(h, g.reshape(1, d), ckv_g.reshape(1, DSA_LATENT), wa, wb)
    return dict(zip([n for n, _, _ in _PROJ_OUT], outs))


def _gla_scores_safe(qs, k, b, a_ref):
    C, SUB = GLA_CHUNK, GLA_SUB
    sub_t = lax.broadcasted_iota(jnp.int32, (SUB, 1), 0)
    out_lane = lax.broadcasted_iota(jnp.int32, (SUB, C), 1)
    for i in range(C // SUB):
        r0 = i * SUB
        b_i = b[r0:r0 + SUB]
        q_i = qs[r0:r0 + SUB]
        if i == 0:
            a_row = jnp.zeros((SUB, C), F32)
        else:
            beta = b[r0 - 1:r0]
            q_t = (q_i * jnp.exp(b_i - beta)).astype(BF16)
            k_t = (k[:r0] * jnp.exp(beta - b[:r0])).astype(BF16)
            k_t = jnp.concatenate([k_t, jnp.zeros((C - r0, GLA_DK), BF16)], axis=0)
            a_row = _nt_dot(q_t, k_t)
        for s in range(SUB):
            e = jnp.exp(jnp.where(sub_t >= s, b_i - b[r0 + s:r0 + s + 1], -jnp.inf))
            col = jnp.sum(q_i * k[r0 + s:r0 + s + 1] * e, axis=-1, keepdims=True)
            a_row = jnp.where(out_lane == r0 + s, col, a_row)
        a_ref[r0:r0 + SUB, :] = a_row


def _gla_kernel(q_ref, k_ref, v_ref, gr_ref, misc_ref, aw_ref, ab_ref, ng_ref, o_ref, st_ref, a_ref):
    C = GLA_CHUNK
    DK, DV = GLA_DK, GLA_DV

    @pl.when(pl.program_id(1) == 0)
    def _():
        st_ref[...] = jnp.zeros_like(st_ref)

    x = _dot_two_piece(misc_ref[0], aw_ref[...]) + ab_ref[...]
    la = (jnp.minimum(x, 0.0) - jnp.log1p(jnp.exp(-jnp.abs(x)))) / GLA_TAU
    row = lax.broadcasted_iota(jnp.int32, (C, C), 0)
    col = lax.broadcasted_iota(jnp.int32, (C, C), 1)
    causal = col <= row
    ones_tri = jnp.where(causal, 1.0, 0.0).astype(BF16)
    la_hi, la_lo = _bf16_pieces(la)
    b_all = (jnp.dot(ones_tri, la_hi, preferred_element_type=F32)
             + jnp.dot(ones_tri, la_lo, preferred_element_type=F32))

    def head(h):
        b = b_all[:, h * DK:(h + 1) * DK]
        k = k_ref[0, :, h * DK:(h + 1) * DK]
        qs = q_ref[0, :, h * DK:(h + 1) * DK] * (DK ** -0.5)
        return b, k, qs, qs * jnp.exp(b)

    bound = jnp.zeros((1, DK), F32)
    for h in range(GLA_HEADS):
        b, k, _, q_in = head(h)
        k_max = jnp.max(jnp.abs(k), axis=0, keepdims=True) * jnp.exp(-b[C - 1:C])
        bound = bound + jnp.maximum(jnp.max(jnp.abs(q_in), axis=0, keepdims=True), 1.0) * k_max
    factored_ok = jnp.sum(bound) < 1e30

    def finish(h, b, k, q_in, scores):
        v = v_ref[0, :, h * DV:(h + 1) * DV]
        st = st_ref[h]
        o = _nt_dot(q_in.astype(BF16), st.astype(BF16))
        o = o + jnp.dot(scores.astype(BF16), v, preferred_element_type=F32)
        o = _rms(o, ng_ref[:, h * DV:(h + 1) * DV])
        gr = gr_ref[0, :, h * DV:(h + 1) * DV].astype(F32)
        o_ref[0, :, h * DV:(h + 1) * DV] = (o * (gr * jax.nn.sigmoid(gr))).astype(o_ref.dtype)
        b_last = b[C - 1:C]
        k_dec = (k * jnp.exp(b_last - b)).astype(BF16)
        st_ref[h] = st * jnp.exp(b_last) + _tn_dot(v, k_dec)

    @pl.when(factored_ok)
    def _():
        for h in range(GLA_HEADS):
            b, k, _, q_in = head(h)
            k_out = k * jnp.exp(-b)
            scores = jnp.where(causal, _nt_dot(q_in.astype(BF16), k_out.astype(BF16)), 0.0)
            finish(h, b, k, q_in, scores)

    @pl.when(jnp.logical_not(factored_ok))
    def _():
        for h in range(GLA_HEADS):
            b, k, qs, q_in = head(h)
            _gla_scores_safe(qs, k, b, a_ref)
            finish(h, b, k, q_in, a_ref[...])


def _gla(z, alpha_w, alpha_b, norm_g, batch, seq):
    C = GLA_CHUNK
    H, DK, DV = GLA_HEADS, GLA_DK, GLA_DV

    def r3(a):
        return a.reshape(batch, seq, a.shape[-1])

    aw = jnp.zeros((LANES, H * DK), F32).at[MISC_GA:MISC_GA + GLA_LOWRANK, :].set(alpha_w)
    tok = lambda w: pl.BlockSpec((1, C, w), lambda b, c: (b, c, 0))
    return pl.pallas_call(
        _gla_kernel,
        grid=(batch, seq // C),
        in_specs=[tok(H * DK), tok(H * DK), tok(H * DV), tok(H * DV), tok(LANES),
                  _const_spec((LANES, H * DK)), _const_spec((1, H * DK)), _const_spec((1, H * DV))],
        out_specs=tok(H * DV),
        out_shape=jax.ShapeDtypeStruct((batch, seq, H * DV), BF16),
        scratch_shapes=[pltpu.VMEM((H, DV, DK), F32), pltpu.VMEM((C, C), F32)],
        compiler_params=_params(("parallel", "arbitrary")),
        name="gla",
    )(r3(z["gq"]), r3(z["gk"]), r3(z["gv"]), r3(z["gr"]), r3(z["misc"]),
      aw, alpha_b.reshape(1, H * DK), norm_g.reshape(1, H * DV))


DSA_NEAR = DSA_TQ // DSA_TK + 1


def _t5_bucket(dist):
    max_exact = REL_BUCKETS // 2
    d = np.maximum(dist, 0)
    steps = (np.log(np.maximum(d, 1).astype(np.float64) / max_exact) / math.log(REL_MAX_DIST / max_exact)
             * (REL_BUCKETS - max_exact))
    live = (d > max_exact) & (steps < REL_BUCKETS - max_exact - 0.5)
    assert np.all(np.abs(steps - np.round(steps))[live] > 1e-4)
    large = np.minimum(max_exact + steps.astype(np.int32), REL_BUCKETS - 1)
    return np.where(d < max_exact, d, large).astype(np.int32)


def _dsa_bias_kernel(rb_ref, bkt_ref, o_ref):
    h = pl.program_id(0)
    bkt = bkt_ref[...]
    far = rb_ref[REL_BUCKETS - 1, h]
    acc = jnp.zeros(bkt.shape, F32)
    for j in range(REL_BUCKETS):
        acc = jnp.where(bkt == j, rb_ref[j, h] - far, acc)
    o_ref[0] = acc * LOG2_E


def _dsa_bias(rel_bias):
    assert DSA_TK + 1 >= REL_MAX_DIST
    s = np.arange(DSA_TK, dtype=np.int32)[:, None]
    t = np.arange(DSA_TQ, dtype=np.int32)[None, :]
    near = [_t5_bucket(t - s + DSA_TK * (1 - j)) for j in range(DSA_NEAR)]
    bkt = jnp.asarray(np.stack([np.full((DSA_TK, DSA_TQ), -1, np.int32)] + near))
    n = DSA_NEAR + 1
    return pl.pallas_call(
        _dsa_bias_kernel,
        grid=(DSA_HEADS,),
        in_specs=[pl.BlockSpec(memory_space=pltpu.SMEM),
                  pl.BlockSpec((n, DSA_TK, DSA_TQ), lambda h: (0, 0, 0))],
        out_specs=pl.BlockSpec((1, n, DSA_TK, DSA_TQ), lambda h: (h, 0, 0, 0)),
        out_shape=jax.ShapeDtypeStruct((DSA_HEADS, n, DSA_TK, DSA_TQ), F32),
        name="dsa_bias",
    )(rel_bias, bkt)


def _dsa_kernel(iqt_ref, mq_ref, mk_ref, ckv_ref, dqt_ref, bias_ref, o_ref,
                sc_ref, scb_ref, lg_ref, wt_ref, acc_ref, ml_ref, *, top_k):
    TQ, TK = DSA_TQ, DSA_TK
    R = TQ // TK
    H = DSA_HEADS
    qi = pl.program_id(1)
    ngroups = qi + 1
    neg_inf = -jnp.inf
    kf = float(top_k)

    wt_ref[...] = mq_ref[0].T * ((IDX_HEADS ** -0.5) * (IDX_DIM ** -0.5))

    def head_rows(ref, h):
        return ref[0, h * LANES:(h + 1) * LANES, :]

    def key_rows(ref, kt):
        return ref[0, pl.ds(pl.multiple_of(kt * TK, TK), TK), :]

    def fold(x, op):
        return op(x.reshape(TK // SUBLANES, SUBLANES, TQ), axis=0)

    k_pos = lax.broadcasted_iota(jnp.int32, (TK, TQ), 0)
    q_pos = lax.broadcasted_iota(jnp.int32, (TK, TQ), 1) + qi * TQ

    def for_tiles(tile_fn):
        def pair(g, carry):
            tile_fn(2 * g)
            tile_fn(2 * g + 1)
            return carry

        lax.fori_loop(0, ngroups // 2, pair, 0)

        @pl.when(ngroups % 2 == 1)
        def _():
            tile_fn(ngroups - 1)

    def score_tile(kt):
        keys = key_rows(mk_ref, kt).astype(BF16)
        acc = jnp.zeros((TK, TQ), F32)
        for h in range(IDX_HEADS):
            d = jnp.dot(keys, head_rows(iqt_ref, h), preferred_element_type=F32)
            acc = acc + jnp.maximum(d, 0.0) * wt_ref[MISC_IW + h:MISC_IW + h + 1, :]
        sc = jnp.where(k_pos + kt * TK <= q_pos, acc, neg_inf)
        sc_ref[kt] = sc
        scb_ref[kt] = sc.astype(BF16)

    for_tiles(score_tile)

    rows = 4 * SUBLANES

    def count_tiles(hits, dtype):
        def pair(g, acc):
            return acc + hits(2 * g) + hits(2 * g + 1)

        acc = lax.fori_loop(0, ngroups // 2, pair, jnp.zeros((rows, TQ), dtype))
        acc = lax.cond(ngroups % 2 == 1, lambda a: a + hits(ngroups - 1), lambda a: a, acc)
        return jnp.sum(acc.astype(F32), axis=0, keepdims=True)

    def count_ge(cand):
        def hits(kt):
            hit = jnp.where(sc_ref[kt] >= cand, 1.0, 0.0)
            return jnp.sum(hit.reshape(TK // rows, rows, TQ), axis=0)

        return count_tiles(hits, F32)

    def count_ge_rounded(cand):
        cand = cand.astype(BF16)

        def hits(kt):
            hit = jnp.where(scb_ref[kt] >= cand, jnp.ones((), BF16), jnp.zeros((), BF16))
            acc = hit[:rows]
            for i in range(1, TK // rows):
                acc = acc + hit[i * rows:(i + 1) * rows]
            return acc

        return count_tiles(hits, BF16)

    def val_of(key):
        return pltpu.bitcast(key ^ ((key >> 31) & 0x7FFFFFFF), F32)

    def bisect(steps, count, value, lo, hi):
        def step(_, state):
            lo, hi, n_lo = state
            mid = lo + ((hi - lo) >> 1)
            n_mid = count(value(mid))
            ok = n_mid >= kf
            return jnp.where(ok, mid, lo), jnp.where(ok, hi, mid), jnp.where(ok, n_mid, n_lo)

        lo, _, n_lo = lax.fori_loop(0, steps, step, (lo, hi, jnp.full((1, TQ), kf, F32)))
        return lo, n_lo

    key16_inf = F32_INF_BITS >> 16
    coarse, _ = bisect(16, count_ge_rounded, lambda k16: val_of((k16 << 16) | ((k16 >> 31) & 0xFFFF)),
                       jnp.full((1, TQ), -key16_inf - 1, jnp.int32), jnp.full((1, TQ), key16_inf + 1, jnp.int32))
    key_inf = F32_INF_BITS
    key_coarse = (coarse << 16) | ((coarse >> 31) & 0xFFFF)
    fine_lo = jnp.maximum(key_coarse - (1 << 15), -key_inf - 1)
    fine_hi = jnp.minimum(key_coarse + (1 << 16), key_inf + 1)
    key_thr, n_ge = bisect(17, count_ge, val_of, fine_lo, fine_hi)
    thr = jnp.maximum(val_of(key_thr), -F32_MAX)
    has_ties = jnp.max(n_ge) > kf

    @pl.when(jnp.logical_not(has_ties))
    def _():
        def body(g, carry):
            for u in range(R):
                kt = g * R + u
                sc_ref[kt] = jnp.where(sc_ref[kt] >= thr, 0.0, neg_inf)
            return carry

        lax.fori_loop(0, ngroups, body, 0)

    @pl.when(has_ties)
    def _():
        def count_gt(g, acc):
            for u in range(R):
                acc = acc + fold(jnp.where(sc_ref[g * R + u] > thr, 1.0, 0.0), jnp.sum)
            return acc

        n_gt = lax.fori_loop(0, ngroups, count_gt, jnp.zeros((SUBLANES, TQ), F32))
        need = kf - jnp.sum(n_gt, axis=0, keepdims=True)
        incl = jnp.where(lax.broadcasted_iota(jnp.int32, (TK, TK), 1)
                         <= lax.broadcasted_iota(jnp.int32, (TK, TK), 0), 1.0, 0.0).astype(BF16)

        def body(g, seen):
            for u in range(R):
                kt = g * R + u
                sc = sc_ref[kt]
                eq = jnp.where(sc == thr, 1.0, 0.0)
                rank = seen + jnp.dot(incl, eq.astype(BF16), preferred_element_type=F32)
                keep = jnp.where(sc > thr, 1.0, jnp.where(rank <= need, eq, 0.0))
                sc_ref[kt] = jnp.where(keep > 0.0, 0.0, neg_inf)
                seen = seen + jnp.sum(eq, axis=0, keepdims=True)
            return seen

        lax.fori_loop(0, ngroups, body, jnp.zeros((1, TQ), F32))

    m_ref, l_ref = ml_ref.at[0], ml_ref.at[1]
    m_ref[...] = jnp.full(m_ref.shape, neg_inf, F32)
    l_ref[...] = jnp.zeros(l_ref.shape, F32)
    acc_ref[...] = jnp.zeros_like(acc_ref)

    def logits_tile(kt):
        kv = key_rows(ckv_ref, kt)
        near = jnp.clip(kt - (R * qi - 1), -1, DSA_NEAR - 1) + 1
        mask = sc_ref[kt]
        for h in range(H):
            lg = jnp.dot(kv, head_rows(dqt_ref, h), preferred_element_type=F32) + (mask + bias_ref[h, near])
            lg_ref[h, kt] = lg
            m_ref[h] = jnp.maximum(m_ref[h], fold(lg, jnp.max))

    for_tiles(logits_tile)
    m_row = [jnp.max(m_ref[h], axis=0, keepdims=True) for h in range(H)]

    def pv_tile(kt):
        kv = key_rows(ckv_ref, kt)
        for h in range(H):
            p = jnp.exp2(lg_ref[h, kt] - m_row[h])
            l_ref[h] += fold(p, jnp.sum)
            acc_ref[h] += _tn_dot(kv, p.astype(BF16))

    for_tiles(pv_tile)
    for h in range(H):
        out_t = acc_ref[h] / jnp.sum(l_ref[h], axis=0, keepdims=True)
        o_ref[0, :, h * LANES:(h + 1) * LANES] = out_t.T.astype(o_ref.dtype)


def _dsa(z, rel_bias, batch, seq):
    TQ, TK = DSA_TQ, DSA_TK
    nq, nk = seq // TQ, seq // TK
    top_k = min(TOPK_MAX, seq // 4)
    H = DSA_HEADS

    def r3(a):
        return a.reshape(batch, seq, a.shape[-1])

    misc = r3(z["misc"])
    return pl.pallas_call(
        functools.partial(_dsa_kernel, top_k=top_k),
        grid=(batch, nq),
        in_specs=[pl.BlockSpec((1, IDX_HEADS * LANES, TQ), lambda b, q: (b * nq + q, 0, 0)),
                  pl.BlockSpec((1, TQ, LANES), lambda b, q: (b, q, 0)),
                  pl.BlockSpec((1, seq, LANES), lambda b, q: (b, 0, 0)),
                  pl.BlockSpec((1, seq, DSA_LATENT), lambda b, q: (b, 0, 0)),
                  pl.BlockSpec((1, H * DSA_LATENT, TQ), lambda b, q: (b * nq + q, 0, 0)),
                  _const_spec((H, DSA_NEAR + 1, TK, TQ))],
        out_specs=pl.BlockSpec((1, TQ, H * DSA_LATENT), lambda b, q: (b, q, 0)),
        out_shape=jax.ShapeDtypeStruct((batch, seq, H * DSA_LATENT), BF16),
        scratch_shapes=[pltpu.VMEM((nk, TK, TQ), F32),
                        pltpu.VMEM((nk, TK, TQ), BF16),
                        pltpu.VMEM((H, nk, TK, TQ), F32),
                        pltpu.VMEM((LANES, TQ), F32),
                        pltpu.VMEM((H, DSA_LATENT, TQ), F32),
                        pltpu.VMEM((2, H, SUBLANES, TQ), F32)],
        compiler_params=_params(("parallel", "arbitrary")),
        name="dsa",
    )(z["iq"], misc, misc, r3(z["ckv"]), z["dq"], _dsa_bias(rel_bias))


def _tail_kernel(h_ref, og_ref, od_ref, gt_ref, p_ref, wa_ref, wd_ref, wo_ref, g2_ref, win_ref, wout_ref,
                 gp_ref, wg_ref, wp_ref, gf_ref, o_ref, act_ref, *, ff):
    d = h_ref.shape[-1]
    ya = jnp.dot(og_ref[...], wa_ref[...], preferred_element_type=F32)
    yd = jnp.dot(od_ref[...], wd_ref[...], preferred_element_type=F32)
    gt = gt_ref[...].astype(F32)
    mix = jax.nn.sigmoid(gt[:, :d]) * ya + jax.nn.sigmoid(gt[:, d:]) * yd
    h = h_ref[...] + jnp.dot(mix.astype(BF16), wo_ref[...], preferred_element_type=F32)
    h = _half_swiglu(h, g2_ref, win_ref, wout_ref, act_ref, ff)
    hn = _rms(h, gp_ref[...]).astype(BF16)
    gate = jax.nn.sigmoid(jnp.dot(hn, wg_ref[...], preferred_element_type=F32))
    emb = jnp.dot(p_ref[...].astype(BF16), wp_ref[...], preferred_element_type=F32)
    o_ref[...] = _rms(h + gate * emb, gf_ref[...])


def _tail(h, og, od, gates, p, wa, wd, wo, g2, w_in, w_out, gp, wg, wp, gf, tm):
    m, d = h.shape
    ff = w_out.shape[0]
    row = lambda w: pl.BlockSpec((tm, w), lambda i: (i, 0))
    vec = lambda g: g.reshape(1, d)
    weights = [w.astype(BF16) for w in (wa, wd, wo)] + [vec(g2), w_in.astype(BF16), w_out.astype(BF16),
                                                         vec(gp), wg.astype(BF16), wp.astype(BF16), vec(gf)]
    return pl.pallas_call(
        functools.partial(_tail_kernel, ff=ff),
        grid=(m // tm,),
        in_specs=[row(d), row(og.shape[1]), row(od.shape[1]), row(2 * d), row(p.shape[1])]
                 + [_const_spec(w.shape) for w in weights],
        out_specs=row(d),
        out_shape=jax.ShapeDtypeStruct((m, d), F32),
        scratch_shapes=[pltpu.VMEM((tm, ff), BF16)],
        compiler_params=_params(("parallel",)),
        name="tail",
    )(h, og, od, gates, p, *weights)


def kernel(x, p, ffn1_norm, ffn1_w_in, ffn1_w_out, mix_norm, mix_w_in, gla_alpha_w, gla_alpha_b, gla_out_norm, gla_w_out, ckv_norm, dsa_w_out, rel_bias, mix_w_out, ffn2_norm, ffn2_w_in, ffn2_w_out, ple_norm, ple_w_gate, ple_w_proj, final_norm):
    batch, seq, d = x.shape
    assert p.shape[0] == 1, "single-layer trunk"
    m = batch * seq
    tm = min(TOKEN_TILE, m)
    h = x.reshape(m, d)
    h = _ffn(h, ffn1_norm[0], ffn1_w_in[0], ffn1_w_out[0], tm)
    z = _inproj(h, mix_norm[0], mix_w_in[0], ckv_norm[0], tm)
    og = _gla(z, gla_alpha_w[0], gla_alpha_b[0], gla_out_norm[0], batch, seq)
    od = _dsa(z, rel_bias, batch, seq)
    h = _tail(h, og.reshape(m, -1), od.reshape(m, -1), z["gates"], p[0].reshape(m, -1),
              gla_w_out[0], dsa_w_out[0], mix_w_out[0], ffn2_norm[0], ffn2_w_in[0], ffn2_w_out[0],
              ple_norm[0], ple_w_gate[0], ple_w_proj[0], final_norm, tm)
    return h.reshape(batch, seq, d)
```

```python
import functools
import math

import numpy as np
import jax
import jax.numpy as jnp
from jax import lax
from jax.experimental import pallas as pl
from jax.experimental.pallas import tpu as pltpu

F32 = jnp.float32
BF16 = jnp.bfloat16

EPS = 1e-6
GLA_HEADS = 4
GLA_DK = 128
GLA_DV = 256
GLA_LOWRANK = 16
GLA_TAU = 16.0
DSA_HEADS = 8
DSA_LATENT = 128
IDX_HEADS = 8
IDX_DIM = 64
TOPK_MAX = 256
REL_BUCKETS = 32
REL_MAX_DIST = 128

LANES = 128
SUBLANES = 8
VMEM_LIMIT = 56 * 1024 * 1024
LOG2_E = math.log2(math.e)
F32_INF_BITS = 0x7F800000
F32_MAX = float(np.finfo(np.float32).max)

TOKEN_TILE = 512
PROJ_COLS = 512
FFN_COLS = 512
GLA_CHUNK = 128
GLA_SUB = 16
GLA_FACTORED_MAX = 1e30
DSA_TQ = 256
DSA_TK = 256

MISC_IK = 0
MISC_GA = IDX_DIM
MISC_IW = IDX_DIM + GLA_LOWRANK


def _nt_dot(a, b):
    return lax.dot_general(a, b, (((1,), (1,)), ((), ())), preferred_element_type=F32)


def _tn_dot(a, b):
    return lax.dot_general(a, b, (((0,), (0,)), ((), ())), preferred_element_type=F32)


def _rms(x, g):
    return x * lax.rsqrt(jnp.mean(x * x, axis=-1, keepdims=True) + EPS) * g


def _bf16_pieces(x):
    hi = x.astype(BF16)
    return hi, (x - hi.astype(F32)).astype(BF16)


def _dot_two_piece(a, b):
    a_hi, a_lo = _bf16_pieces(a)
    b_hi, b_lo = _bf16_pieces(b)
    dot = functools.partial(jnp.dot, preferred_element_type=F32)
    return dot(a_hi, b_hi) + dot(a_lo, b_hi) + dot(a_hi, b_lo)


def _const_spec(shape):
    nd = len(shape)
    return pl.BlockSpec(shape, lambda *_: (0,) * nd, pipeline_mode=pl.Buffered(1))


def _params(sem):
    return pltpu.CompilerParams(dimension_semantics=sem, vmem_limit_bytes=VMEM_LIMIT)


def _half_swiglu(x, g_ref, win_ref, wout_ref, act_ref, ff):
    xn = _rms(x, g_ref[...]).astype(BF16)
    for c0 in range(0, ff, FFN_COLS):
        cw = min(FFN_COLS, ff - c0)
        gate = jnp.dot(xn, win_ref[:, c0:c0 + cw], preferred_element_type=F32)
        up = jnp.dot(xn, win_ref[:, ff + c0:ff + c0 + cw], preferred_element_type=F32)
        act_ref[:, c0:c0 + cw] = (gate * jax.nn.sigmoid(gate) * up).astype(BF16)
    return x + 0.5 * jnp.dot(act_ref[...], wout_ref[...], preferred_element_type=F32)


def _ffn_kernel(h_ref, g_ref, win_ref, wout_ref, o_ref, act_ref, *, ff):
    o_ref[...] = _half_swiglu(h_ref[...], g_ref, win_ref, wout_ref, act_ref, ff)


def _ffn(h, g, w_in, w_out, tm):
    m, d = h.shape
    ff = w_out.shape[0]
    assert ff % LANES == 0
    return pl.pallas_call(
        functools.partial(_ffn_kernel, ff=ff),
        grid=(m // tm,),
        in_specs=[pl.BlockSpec((tm, d), lambda i: (i, 0)),
                  _const_spec((1, d)),
                  _const_spec((d, 2 * ff)),
                  _const_spec((ff, d))],
        out_specs=pl.BlockSpec((tm, d), lambda i: (i, 0)),
        out_shape=jax.ShapeDtypeStruct((m, d), F32),
        scratch_shapes=[pltpu.VMEM((tm, ff), BF16)],
        compiler_params=_params(("parallel",)),
        name="ffn",
    )(h, g.reshape(1, d), w_in.astype(BF16), w_out.astype(BF16))


_PROJ_OUT = (("gq", GLA_HEADS * GLA_DK, F32), ("gk", GLA_HEADS * GLA_DK, F32),
             ("gv", GLA_HEADS * GLA_DV, BF16), ("gr", GLA_HEADS * GLA_DV, BF16),
             ("dq", DSA_HEADS * DSA_LATENT, BF16), ("iq", IDX_HEADS * IDX_DIM, BF16),
             ("gates", None, BF16), ("ckv", DSA_LATENT, BF16), ("misc", LANES, F32))
_PROJ_TRANSPOSED = ("dq", "iq")


def _inproj_kernel(h_ref, g_ref, ckvg_ref, wa_ref, wb_ref, *out_refs, widths):
    xn = _rms(h_ref[...], g_ref[...]).astype(BF16)
    w_ref, col = wa_ref, 0
    for (name, _, dt), width, o_ref in zip(_PROJ_OUT, widths, out_refs):
        if col == wa_ref.shape[1] and w_ref is wa_ref:
            w_ref, col = wb_ref, 0
        for c0 in range(0, width, PROJ_COLS):
            cw = min(PROJ_COLS, width - c0)
            z = jnp.dot(xn, w_ref[:, col + c0:col + c0 + cw], preferred_element_type=F32)
            if name == "ckv":
                z = _rms(z, ckvg_ref[...])
            elif name == "dq":
                z = z * (DSA_LATENT ** -0.5 * LOG2_E)
            if name in _PROJ_TRANSPOSED:
                zt = z.T
                for j in range(z.shape[0] // DSA_TQ):
                    block = zt[:, j * DSA_TQ:(j + 1) * DSA_TQ].astype(dt)
                    if name == "iq":
                        assert c0 == 0 and cw == width
                        for h in range(IDX_HEADS):
                            o_ref[j, h * LANES:h * LANES + IDX_DIM, :] = block[h * IDX_DIM:(h + 1) * IDX_DIM]
                            o_ref[j, h * LANES + IDX_DIM:(h + 1) * LANES, :] = jnp.zeros(
                                (LANES - IDX_DIM, DSA_TQ), dt)
                    else:
                        o_ref[j, c0:c0 + cw, :] = block
            else:
                o_ref[:, c0:c0 + cw] = z.astype(dt)
        col += width


def _inproj(h, g, w_in, ckv_g, tm):
    m, d = h.shape
    sizes = (GLA_HEADS * GLA_DK, GLA_HEADS * GLA_DK, GLA_HEADS * GLA_DV, GLA_HEADS * GLA_DV, GLA_LOWRANK,
             DSA_HEADS * DSA_LATENT, DSA_LATENT, IDX_HEADS * IDX_DIM, IDX_DIM, IDX_HEADS, 2 * d)
    pts = np.cumsum(sizes)[:-1].tolist()
    _, _, _, _, ga, dq, dkv, iq, ik, iw, gates = jnp.split(w_in, pts, axis=-1)
    misc = jnp.pad(jnp.concatenate([ik, ga, iw], axis=-1),
                   ((0, 0), (0, LANES - IDX_DIM - GLA_LOWRANK - IDX_HEADS)))
    wa = w_in[:, :pts[3]].astype(BF16)
    wb = jnp.concatenate([dq, iq, gates, dkv, misc], axis=-1).astype(BF16)
    widths = tuple(wd if wd is not None else 2 * d for _, wd, _ in _PROJ_OUT)
    assert wa.shape[1] == sum(widths[:4]) and wa.shape[1] + wb.shape[1] == sum(widths)
    out_widths = tuple(IDX_HEADS * LANES if name == "iq" else wd for wd, (name, _, _) in zip(widths, _PROJ_OUT))
    outs = pl.pallas_call(
        functools.partial(_inproj_kernel, widths=widths),
        grid=(m // tm,),
        in_specs=[pl.BlockSpec((tm, d), lambda i: (i, 0)),
                  _const_spec((1, d)),
                  _const_spec((1, DSA_LATENT)),
                  _const_spec(wa.shape),
                  _const_spec(wb.shape)],
        out_specs=[pl.BlockSpec((tm // DSA_TQ, wd, DSA_TQ), lambda i: (i, 0, 0)) if name in _PROJ_TRANSPOSED
                   else pl.BlockSpec((tm, wd), lambda i: (i, 0)) for wd, (name, _, _) in zip(out_widths, _PROJ_OUT)],
        out_shape=[jax.ShapeDtypeStruct((m // DSA_TQ, wd, DSA_TQ) if name in _PROJ_TRANSPOSED else (m, wd), dt)
                   for wd, (name, _, dt) in zip(out_widths, _PROJ_OUT)],
        compiler_params=_params(("parallel",)),
        name="inproj",
    )(h, g.reshape(1, d), ckv_g.reshape(1, DSA_LATENT), wa, wb)
    return dict(zip([n for n, _, _ in _PROJ_OUT], outs))


def _gla_scores_safe(qs, k, b, a_ref):
    C, SUB = GLA_CHUNK, GLA_SUB
    sub_t = lax.broadcasted_iota(jnp.int32, (SUB, 1), 0)
    out_lane = lax.broadcasted_iota(jnp.int32, (SUB, C), 1)
    for i in range(C // SUB):
        r0 = i * SUB
        b_i = b[r0:r0 + SUB]
        q_i = qs[r0:r0 + SUB]
        if i == 0:
            a_row = jnp.zeros((SUB, C), F32)
        else:
            beta = b[r0 - 1:r0]
            q_t = (q_i * jnp.exp(b_i - beta)).astype(BF16)
            k_t = (k[:r0] * jnp.exp(beta - b[:r0])).astype(BF16)
            k_t = jnp.concatenate([k_t, jnp.zeros((C - r0, GLA_DK), BF16)], axis=0)
            a_row = _nt_dot(q_t, k_t)
        for s in range(SUB):
            e = jnp.exp(jnp.where(sub_t >= s, b_i - b[r0 + s:r0 + s + 1], -jnp.inf))
            col = jnp.sum(q_i * k[r0 + s:r0 + s + 1] * e, axis=-1, keepdims=True)
            a_row = jnp.where(out_lane == r0 + s, col, a_row)
        a_ref[r0:r0 + SUB, :] = a_row


def _gla_kernel(q_ref, k_ref, v_ref, gr_ref, misc_ref, aw_ref, ab_ref, ng_ref, o_ref, st_ref, a_ref):
    C = GLA_CHUNK
    DK, DV = GLA_DK, GLA_DV

    @pl.when(pl.program_id(1) == 0)
    def _():
        st_ref[...] = jnp.zeros_like(st_ref)

    x = _dot_two_piece(misc_ref[0], aw_ref[...]) + ab_ref[...]
    la = (jnp.minimum(x, 0.0) - jnp.log1p(jnp.exp(-jnp.abs(x)))) / GLA_TAU
    row = lax.broadcasted_iota(jnp.int32, (C, C), 0)
    col = lax.broadcasted_iota(jnp.int32, (C, C), 1)
    causal = col <= row
    ones_tri = jnp.where(causal, 1.0, 0.0).astype(BF16)
    la_hi, la_lo = _bf16_pieces(la)
    b_all = (jnp.dot(ones_tri, la_hi, preferred_element_type=F32)
             + jnp.dot(ones_tri, la_lo, preferred_element_type=F32))

    def head(h):
        b = b_all[:, h * DK:(h + 1) * DK]
        k = k_ref[0, :, h * DK:(h + 1) * DK]
        qs = q_ref[0, :, h * DK:(h + 1) * DK] * (DK ** -0.5)
        return b, k, qs, qs * jnp.exp(b)

    bound = jnp.zeros((1, DK), F32)
    for h in range(GLA_HEADS):
        b, k, _, q_in = head(h)
        k_max = jnp.max(jnp.abs(k), axis=0, keepdims=True) * jnp.exp(-b[C - 1:C])
        bound = bound + jnp.maximum(jnp.max(jnp.abs(q_in), axis=0, keepdims=True), 1.0) * k_max
    factored_ok = jnp.sum(bound) < GLA_FACTORED_MAX

    def finish(h, b, k, q_in, scores):
        v = v_ref[0, :, h * DV:(h + 1) * DV]
        st = st_ref[h]
        o = _nt_dot(q_in.astype(BF16), st.astype(BF16))
        o = o + jnp.dot(scores.astype(BF16), v, preferred_element_type=F32)
        o = _rms(o, ng_ref[:, h * DV:(h + 1) * DV])
        gr = gr_ref[0, :, h * DV:(h + 1) * DV].astype(F32)
        o_ref[0, :, h * DV:(h + 1) * DV] = (o * (gr * jax.nn.sigmoid(gr))).astype(o_ref.dtype)
        b_last = b[C - 1:C]
        k_dec = (k * jnp.exp(b_last - b)).astype(BF16)
        st_ref[h] = st * jnp.exp(b_last) + _tn_dot(v, k_dec)

    @pl.when(factored_ok)
    def _():
        for h in range(GLA_HEADS):
            b, k, _, q_in = head(h)
            k_out = k * jnp.exp(-b)
            scores = jnp.where(causal, _nt_dot(q_in.astype(BF16), k_out.astype(BF16)), 0.0)
            finish(h, b, k, q_in, scores)

    @pl.when(jnp.logical_not(factored_ok))
    def _():
        for h in range(GLA_HEADS):
            b, k, qs, q_in = head(h)
            _gla_scores_safe(qs, k, b, a_ref)
            finish(h, b, k, q_in, a_ref[...])


def _gla(z, alpha_w, alpha_b, norm_g, batch, seq):
    C = GLA_CHUNK
    H, DK, DV = GLA_HEADS, GLA_DK, GLA_DV

    def r3(a):
        return a.reshape(batch, seq, a.shape[-1])

    aw = jnp.zeros((LANES, H * DK), F32).at[MISC_GA:MISC_GA + GLA_LOWRANK, :].set(alpha_w)
    tok = lambda w: pl.BlockSpec((1, C, w), lambda b, c: (b, c, 0))
    return pl.pallas_call(
        _gla_kernel,
        grid=(batch, seq // C),
        in_specs=[tok(H * DK), tok(H * DK), tok(H * DV), tok(H * DV), tok(LANES),
                  _const_spec((LANES, H * DK)), _const_spec((1, H * DK)), _const_spec((1, H * DV))],
        out_specs=tok(H * DV),
        out_shape=jax.ShapeDtypeStruct((batch, seq, H * DV), BF16),
        scratch_shapes=[pltpu.VMEM((H, DV, DK), F32), pltpu.VMEM((C, C), F32)],
        compiler_params=_params(("parallel", "arbitrary")),
        name="gla",
    )(r3(z["gq"]), r3(z["gk"]), r3(z["gv"]), r3(z["gr"]), r3(z["misc"]),
      aw, alpha_b.reshape(1, H * DK), norm_g.reshape(1, H * DV))


DSA_NEAR = DSA_TQ // DSA_TK + 1


def _t5_bucket(dist):
    max_exact = REL_BUCKETS // 2
    d = np.maximum(dist, 0)
    steps = (np.log(np.maximum(d, 1).astype(np.float64) / max_exact) / math.log(REL_MAX_DIST / max_exact)
             * (REL_BUCKETS - max_exact))
    live = (d > max_exact) & (steps < REL_BUCKETS - max_exact - 0.5)
    assert np.all(np.abs(steps - np.round(steps))[live] > 1e-4)
    large = np.minimum(max_exact + steps.astype(np.int32), REL_BUCKETS - 1)
    return np.where(d < max_exact, d, large).astype(np.int32)


def _dsa_bias_kernel(rb_ref, bkt_ref, o_ref):
    h = pl.program_id(0)
    bkt = bkt_ref[...]
    far = rb_ref[REL_BUCKETS - 1, h]
    acc = jnp.zeros(bkt.shape, F32)
    for j in range(REL_BUCKETS):
        acc = jnp.where(bkt == j, rb_ref[j, h] - far, acc)
    o_ref[0, 0] = jnp.zeros(bkt.shape[1:], F32)
    o_ref[0, 1:] = acc * LOG2_E


def _dsa_bias(rel_bias):
    assert DSA_TK + 1 >= REL_MAX_DIST
    s = np.arange(DSA_TK, dtype=np.int32)[:, None]
    t = np.arange(DSA_TQ, dtype=np.int32)[None, :]
    near = [_t5_bucket(t - s + DSA_TK * (1 - j)) for j in range(DSA_NEAR)]
    bkt = jnp.asarray(np.stack(near))
    n = DSA_NEAR + 1
    return pl.pallas_call(
        _dsa_bias_kernel,
        grid=(DSA_HEADS,),
        in_specs=[pl.BlockSpec(memory_space=pltpu.SMEM),
                  pl.BlockSpec((DSA_NEAR, DSA_TK, DSA_TQ), lambda h: (0, 0, 0))],
        out_specs=pl.BlockSpec((1, n, DSA_TK, DSA_TQ), lambda h: (h, 0, 0, 0)),
        out_shape=jax.ShapeDtypeStruct((DSA_HEADS, n, DSA_TK, DSA_TQ), F32),
        name="dsa_bias",
    )(rel_bias, bkt)


def _dsa_kernel(iqt_ref, mq_ref, mk_ref, ckv_ref, dqt_ref, bias_ref, o_ref,
                sc_ref, scb_ref, lg_ref, wt_ref, acc_ref, ml_ref, *, top_k):
    TQ, TK = DSA_TQ, DSA_TK
    R = TQ // TK
    H = DSA_HEADS
    qi = pl.program_id(1)
    ngroups = qi + 1
    neg_inf = -jnp.inf
    kf = float(top_k)

    wt_ref[...] = mq_ref[0].T * ((IDX_HEADS ** -0.5) * (IDX_DIM ** -0.5))

    def head_rows(ref, h):
        return ref[0, h * LANES:(h + 1) * LANES, :]

    def key_rows(ref, kt):
        return ref[0, pl.ds(pl.multiple_of(kt * TK, TK), TK), :]

    def fold(x, op):
        return op(x.reshape(TK // SUBLANES, SUBLANES, TQ), axis=0)

    k_pos = lax.broadcasted_iota(jnp.int32, (TK, TQ), 0)
    q_pos = lax.broadcasted_iota(jnp.int32, (TK, TQ), 1) + qi * TQ

    def for_tiles(tile_fn):
        def pair(g, carry):
            tile_fn(2 * g)
            tile_fn(2 * g + 1)
            return carry

        lax.fori_loop(0, ngroups // 2, pair, 0)

        @pl.when(ngroups % 2 == 1)
        def _():
            tile_fn(ngroups - 1)

    def score_tile(kt):
        keys = key_rows(mk_ref, kt).astype(BF16)
        acc = jnp.zeros((TK, TQ), F32)
        for h in range(IDX_HEADS):
            d = jnp.dot(keys, head_rows(iqt_ref, h), preferred_element_type=F32)
            acc = acc + jnp.maximum(d, 0.0) * wt_ref[MISC_IW + h:MISC_IW + h + 1, :]
        sc = jnp.where(k_pos + kt * TK <= q_pos, acc, neg_inf)
        sc_ref[kt] = sc
        scb_ref[kt] = sc.astype(BF16)

    for_tiles(score_tile)

    rows = 4 * SUBLANES

    def count_tiles(hits, dtype):
        def pair(g, acc):
            return acc + hits(2 * g) + hits(2 * g + 1)

        acc = lax.fori_loop(0, ngroups // 2, pair, jnp.zeros((rows, TQ), dtype))
        acc = lax.cond(ngroups % 2 == 1, lambda a: a + hits(ngroups - 1), lambda a: a, acc)
        return jnp.sum(acc.astype(F32), axis=0, keepdims=True)

    def count_ge(cand):
        def hits(kt):
            hit = jnp.where(sc_ref[kt] >= cand, 1.0, 0.0)
            return jnp.sum(hit.reshape(TK // rows, rows, TQ), axis=0)

        return count_tiles(hits, F32)

    def count_ge_rounded(cand):
        cand = cand.astype(BF16)

        def hits(kt):
            hit = jnp.where(scb_ref[kt] >= cand, jnp.ones((), BF16), jnp.zeros((), BF16))
            acc = hit[:rows]
            for i in range(1, TK // rows):
                acc = acc + hit[i * rows:(i + 1) * rows]
            return acc

        return count_tiles(hits, BF16)

    def val_of(key):
        return pltpu.bitcast(key ^ ((key >> 31) & 0x7FFFFFFF), F32)

    def bisect(span, count, value, lo, hi):
        def step(_, state):
            lo, hi, n_lo = state
            mid = lo + ((hi - lo) >> 1)
            n_mid = count(value(mid))
            ok = n_mid >= kf
            return jnp.where(ok, mid, lo), jnp.where(ok, hi, mid), jnp.where(ok, n_mid, n_lo)

        steps = (span - 1).bit_length()
        lo, _, n_lo = lax.fori_loop(0, steps, step, (lo, hi, jnp.full((1, TQ), kf, F32)))
        return lo, n_lo

    key16_inf = F32_INF_BITS >> 16
    coarse, _ = bisect(2 * key16_inf + 2, count_ge_rounded, lambda k16: val_of((k16 << 16) | ((k16 >> 31) & 0xFFFF)),
                       jnp.full((1, TQ), -key16_inf - 1, jnp.int32), jnp.full((1, TQ), key16_inf + 1, jnp.int32))
    key_inf = F32_INF_BITS
    key_coarse = (coarse << 16) | ((coarse >> 31) & 0xFFFF)
    fine_lo = jnp.maximum(key_coarse - (1 << 15), -key_inf - 1)
    fine_hi = jnp.minimum(key_coarse + (1 << 16), key_inf + 1)
    key_thr, n_ge = bisect((1 << 16) + (1 << 15), count_ge, val_of, fine_lo, fine_hi)
    thr = jnp.maximum(val_of(key_thr), -F32_MAX)
    has_ties = jnp.max(n_ge) > kf

    @pl.when(jnp.logical_not(has_ties))
    def _():
        def body(g, carry):
            for u in range(R):
                kt = g * R + u
                sc_ref[kt] = jnp.where(sc_ref[kt] >= thr, 0.0, neg_inf)
            return carry

        lax.fori_loop(0, ngroups, body, 0)

    @pl.when(has_ties)
    def _():
        def count_gt(g, acc):
            for u in range(R):
                acc = acc + fold(jnp.where(sc_ref[g * R + u] > thr, 1.0, 0.0), jnp.sum)
            return acc

        n_gt = lax.fori_loop(0, ngroups, count_gt, jnp.zeros((SUBLANES, TQ), F32))
        need = kf - jnp.sum(n_gt, axis=0, keepdims=True)
        incl = jnp.where(lax.broadcasted_iota(jnp.int32, (TK, TK), 1)
                         <= lax.broadcasted_iota(jnp.int32, (TK, TK), 0), 1.0, 0.0).astype(BF16)

        def body(g, seen):
            for u in range(R):
                kt = g * R + u
                sc = sc_ref[kt]
                eq = jnp.where(sc == thr, 1.0, 0.0)
                rank = seen + jnp.dot(incl, eq.astype(BF16), preferred_element_type=F32)
                keep = jnp.where(sc > thr, 1.0, jnp.where(rank <= need, eq, 0.0))
                sc_ref[kt] = jnp.where(keep > 0.0, 0.0, neg_inf)
                seen = seen + jnp.sum(eq, axis=0, keepdims=True)
            return seen

        lax.fori_loop(0, ngroups, body, jnp.zeros((1, TQ), F32))

    m_ref, l_ref = ml_ref.at[0], ml_ref.at[1]
    m_ref[...] = jnp.full(m_ref.shape, neg_inf, F32)
    l_ref[...] = jnp.zeros(l_ref.shape, F32)
    acc_ref[...] = jnp.zeros_like(acc_ref)

    def logits_tile(kt):
        kv = key_rows(ckv_ref, kt)
        near = jnp.clip(kt - (R * qi - 1), -1, DSA_NEAR - 1) + 1
        mask = sc_ref[kt]
        for h in range(H):
            lg = jnp.dot(kv, head_rows(dqt_ref, h), preferred_element_type=F32) + (mask + bias_ref[h, near])
            lg_ref[h, kt] = lg
            m_ref[h] = jnp.maximum(m_ref[h], fold(lg, jnp.max))

    for_tiles(logits_tile)
    m_row = [jnp.max(m_ref[h], axis=0, keepdims=True) for h in range(H)]

    def pv_tile(kt):
        kv = key_rows(ckv_ref, kt)
        for h in range(H):
            p = jnp.exp2(lg_ref[h, kt] - m_row[h])
            l_ref[h] += fold(p, jnp.sum)
            acc_ref[h] += _tn_dot(kv, p.astype(BF16))

    for_tiles(pv_tile)
    for h in range(H):
        out_t = acc_ref[h] / jnp.sum(l_ref[h], axis=0, keepdims=True)
        o_ref[0, :, h * LANES:(h + 1) * LANES] = out_t.T.astype(o_ref.dtype)


def _dsa(z, rel_bias, batch, seq):
    TQ, TK = DSA_TQ, DSA_TK
    nq, nk = seq // TQ, seq // TK
    top_k = min(TOPK_MAX, seq // 4)
    H = DSA_HEADS

    def r3(a):
        return a.reshape(batch, seq, a.shape[-1])

    misc = r3(z["misc"])
    return pl.pallas_call(
        functools.partial(_dsa_kernel, top_k=top_k),
        grid=(batch, nq),
        in_specs=[pl.BlockSpec((1, IDX_HEADS * LANES, TQ), lambda b, q: (b * nq + q, 0, 0)),
                  pl.BlockSpec((1, TQ, LANES), lambda b, q: (b, q, 0)),
                  pl.BlockSpec((1, seq, LANES), lambda b, q: (b, 0, 0)),
                  pl.BlockSpec((1, seq, DSA_LATENT), lambda b, q: (b, 0, 0)),
                  pl.BlockSpec((1, H * DSA_LATENT, TQ), lambda b, q: (b * nq + q, 0, 0)),
                  _const_spec((H, DSA_NEAR + 1, TK, TQ))],
        out_specs=pl.BlockSpec((1, TQ, H * DSA_LATENT), lambda b, q: (b, q, 0)),
        out_shape=jax.ShapeDtypeStruct((batch, seq, H * DSA_LATENT), BF16),
        scratch_shapes=[pltpu.VMEM((nk, TK, TQ), F32),
                        pltpu.VMEM((nk, TK, TQ), BF16),
                        pltpu.VMEM((H, nk, TK, TQ), F32),
                        pltpu.VMEM((LANES, TQ), F32),
                        pltpu.VMEM((H, DSA_LATENT, TQ), F32),
                        pltpu.VMEM((2, H, SUBLANES, TQ), F32)],
        compiler_params=_params(("parallel", "arbitrary")),
        name="dsa",
    )(z["iq"], misc, misc, r3(z["ckv"]), z["dq"], _dsa_bias(rel_bias))


def _tail_kernel(h_ref, og_ref, od_ref, gt_ref, p_ref, wa_ref, wd_ref, wo_ref, g2_ref, win_ref, wout_ref,
                 gp_ref, wg_ref, wp_ref, gf_ref, o_ref, act_ref, *, ff):
    d = h_ref.shape[-1]
    ya = jnp.dot(og_ref[...], wa_ref[...], preferred_element_type=F32)
    yd = jnp.dot(od_ref[...], wd_ref[...], preferred_element_type=F32)
    gt = gt_ref[...].astype(F32)
    mix = jax.nn.sigmoid(gt[:, :d]) * ya + jax.nn.sigmoid(gt[:, d:]) * yd
    h = h_ref[...] + jnp.dot(mix.astype(BF16), wo_ref[...], preferred_element_type=F32)
    h = _half_swiglu(h, g2_ref, win_ref, wout_ref, act_ref, ff)
    hn = _rms(h, gp_ref[...]).astype(BF16)
    gate = jax.nn.sigmoid(jnp.dot(hn, wg_ref[...], preferred_element_type=F32))
    emb = jnp.dot(p_ref[...].astype(BF16), wp_ref[...], preferred_element_type=F32)
    o_ref[...] = _rms(h + gate * emb, gf_ref[...])


def _tail(h, og, od, gates, p, wa, wd, wo, g2, w_in, w_out, gp, wg, wp, gf, tm):
    m, d = h.shape
    ff = w_out.shape[0]
    row = lambda w: pl.BlockSpec((tm, w), lambda i: (i, 0))
    vec = lambda g: g.reshape(1, d)
    weights = [w.astype(BF16) for w in (wa, wd, wo)] + [vec(g2), w_in.astype(BF16), w_out.astype(BF16),
                                                         vec(gp), wg.astype(BF16), wp.astype(BF16), vec(gf)]
    return pl.pallas_call(
        functools.partial(_tail_kernel, ff=ff),
        grid=(m // tm,),
        in_specs=[row(d), row(og.shape[1]), row(od.shape[1]), row(2 * d), row(p.shape[1])]
                 + [_const_spec(w.shape) for w in weights],
        out_specs=row(d),
        out_shape=jax.ShapeDtypeStruct((m, d), F32),
        scratch_shapes=[pltpu.VMEM((tm, ff), BF16)],
        compiler_params=_params(("parallel",)),
        name="tail",
    )(h, og, od, gates, p, *weights)


def kernel(x, p, ffn1_norm, ffn1_w_in, ffn1_w_out, mix_norm, mix_w_in, gla_alpha_w, gla_alpha_b, gla_out_norm, gla_w_out, ckv_norm, dsa_w_out, rel_bias, mix_w_out, ffn2_norm, ffn2_w_in, ffn2_w_out, ple_norm, ple_w_gate, ple_w_proj, final_norm):
    batch, seq, d = x.shape
    assert p.shape[0] == 1, "single-layer trunk"
    m = batch * seq
    tm = min(TOKEN_TILE, m)
    h = x.reshape(m, d)
    h = _ffn(h, ffn1_norm[0], ffn1_w_in[0], ffn1_w_out[0], tm)
    z = _inproj(h, mix_norm[0], mix_w_in[0], ckv_norm[0], tm)
    og = _gla(z, gla_alpha_w[0], gla_alpha_b[0], gla_out_norm[0], batch, seq)
    od = _dsa(z, rel_bias, batch, seq)
    h = _tail(h, og.reshape(m, -1), od.reshape(m, -1), z["gates"], p[0].reshape(m, -1),
              gla_w_out[0], dsa_w_out[0], mix_w_out[0], ffn2_norm[0], ffn2_w_in[0], ffn2_w_out[0],
              ple_norm[0], ple_w_gate[0], ple_w_proj[0], final_norm, tm)
    return h.reshape(batch, seq, d)
```

```python
import functools
import math

import numpy as np
import jax
import jax.numpy as jnp
from jax import lax
from jax.experimental import pallas as pl
from jax.experimental.pallas import tpu as pltpu

F32 = jnp.float32
BF16 = jnp.bfloat16

EPS = 1e-6
GLA_HEADS = 4
GLA_DK = 128
GLA_DV = 256
GLA_LOWRANK = 16
GLA_TAU = 16.0
DSA_HEADS = 8
DSA_LATENT = 128
IDX_HEADS = 8
IDX_DIM = 64
TOPK_MAX = 256
REL_BUCKETS = 32
REL_MAX_DIST = 128

LANES = 128
SUBLANES = 8
VMEM_LIMIT = 56 * 1024 * 1024
LOG2_E = math.log2(math.e)
F32_INF_BITS = 0x7F800000
F32_MAX = float(np.finfo(np.float32).max)

TOKEN_TILE = 512
PROJ_COLS = 512
FFN_COLS = 512
GLA_CHUNK = 128
GLA_SUB = 16
GLA_FACTORED_MAX = 1e30
DSA_TQ = 256
DSA_TK = 256

MISC_IK = 0
MISC_GA = IDX_DIM
MISC_IW = IDX_DIM + GLA_LOWRANK


def _nt_dot(a, b):
    return lax.dot_general(a, b, (((1,), (1,)), ((), ())), preferred_element_type=F32)


def _tn_dot(a, b):
    return lax.dot_general(a, b, (((0,), (0,)), ((), ())), preferred_element_type=F32)


def _rms(x, g):
    return x * lax.rsqrt(jnp.mean(x * x, axis=-1, keepdims=True) + EPS) * g


def _bf16_pieces(x):
    hi = x.astype(BF16)
    return hi, (x - hi.astype(F32)).astype(BF16)


def _dot_two_piece(a, b):
    a_hi, a_lo = _bf16_pieces(a)
    b_hi, b_lo = _bf16_pieces(b)
    dot = functools.partial(jnp.dot, preferred_element_type=F32)
    return dot(a_hi, b_hi) + dot(a_lo, b_hi) + dot(a_hi, b_lo)


def _const_spec(shape):
    nd = len(shape)
    return pl.BlockSpec(shape, lambda *_: (0,) * nd, pipeline_mode=pl.Buffered(1))


def _params(sem):
    return pltpu.CompilerParams(dimension_semantics=sem, vmem_limit_bytes=VMEM_LIMIT)


def _half_swiglu(x, g_ref, win_ref, wout_ref, act_ref, ff):
    xn = _rms(x, g_ref[...]).astype(BF16)
    for c0 in range(0, ff, FFN_COLS):
        cw = min(FFN_COLS, ff - c0)
        gate = jnp.dot(xn, win_ref[:, c0:c0 + cw], preferred_element_type=F32)
        up = jnp.dot(xn, win_ref[:, ff + c0:ff + c0 + cw], preferred_element_type=F32)
        act_ref[:, c0:c0 + cw] = (gate * jax.nn.sigmoid(gate) * up).astype(BF16)
    return x + 0.5 * jnp.dot(act_ref[...], wout_ref[...], preferred_element_type=F32)


def _ffn_kernel(h_ref, g_ref, win_ref, wout_ref, o_ref, act_ref, *, ff):
    o_ref[...] = _half_swiglu(h_ref[...], g_ref, win_ref, wout_ref, act_ref, ff)


def _ffn(h, g, w_in, w_out, tm):
    m, d = h.shape
    ff = w_out.shape[0]
    assert ff % LANES == 0
    return pl.pallas_call(
        functools.partial(_ffn_kernel, ff=ff),
        grid=(m // tm,),
        in_specs=[pl.BlockSpec((tm, d), lambda i: (i, 0)),
                  _const_spec((1, d)),
                  _const_spec((d, 2 * ff)),
                  _const_spec((ff, d))],
        out_specs=pl.BlockSpec((tm, d), lambda i: (i, 0)),
        out_shape=jax.ShapeDtypeStruct((m, d), F32),
        scratch_shapes=[pltpu.VMEM((tm, ff), BF16)],
        compiler_params=_params(("parallel",)),
        name="ffn",
    )(h, g.reshape(1, d), w_in.astype(BF16), w_out.astype(BF16))


_PROJ_OUT = (("gq", GLA_HEADS * GLA_DK, F32), ("gk", GLA_HEADS * GLA_DK, F32),
             ("gv", GLA_HEADS * GLA_DV, BF16), ("gr", GLA_HEADS * GLA_DV, BF16),
             ("dq", DSA_HEADS * DSA_LATENT, BF16), ("iq", IDX_HEADS * IDX_DIM, BF16),
             ("gates", None, BF16), ("ckv", DSA_LATENT, BF16), ("misc", LANES, F32))
_PROJ_TRANSPOSED = ("dq", "iq")


def _inproj_kernel(h_ref, g_ref, ckvg_ref, wa_ref, wb_ref, *out_refs, widths):
    xn = _rms(h_ref[...], g_ref[...]).astype(BF16)
    w_ref, col = wa_ref, 0
    for (name, _, dt), width, o_ref in zip(_PROJ_OUT, widths, out_refs):
        if col == wa_ref.shape[1] and w_ref is wa_ref:
            w_ref, col = wb_ref, 0
        for c0 in range(0, width, PROJ_COLS):
            cw = min(PROJ_COLS, width - c0)
            z = jnp.dot(xn, w_ref[:, col + c0:col + c0 + cw], preferred_element_type=F32)
            if name == "ckv":
                z = _rms(z, ckvg_ref[...])
            elif name == "dq":
                z = z * (DSA_LATENT ** -0.5 * LOG2_E)
            if name in _PROJ_TRANSPOSED:
                zt = z.T
                for j in range(z.shape[0] // DSA_TQ):
                    block = zt[:, j * DSA_TQ:(j + 1) * DSA_TQ].astype(dt)
                    if name == "iq":
                        assert c0 == 0 and cw == width
                        for h in range(IDX_HEADS):
                            o_ref[j, h * LANES:h * LANES + IDX_DIM, :] = block[h * IDX_DIM:(h + 1) * IDX_DIM]
                            o_ref[j, h * LANES + IDX_DIM:(h + 1) * LANES, :] = jnp.zeros(
                                (LANES - IDX_DIM, DSA_TQ), dt)
                    else:
                        o_ref[j, c0:c0 + cw, :] = block
            else:
                o_ref[:, c0:c0 + cw] = z.astype(dt)
        col += width


def _inproj(h, g, w_in, ckv_g, tm):
    m, d = h.shape
    sizes = (GLA_HEADS * GLA_DK, GLA_HEADS * GLA_DK, GLA_HEADS * GLA_DV, GLA_HEADS * GLA_DV, GLA_LOWRANK,
             DSA_HEADS * DSA_LATENT, DSA_LATENT, IDX_HEADS * IDX_DIM, IDX_DIM, IDX_HEADS, 2 * d)
    pts = np.cumsum(sizes)[:-1].tolist()
    _, _, _, _, ga, dq, dkv, iq, ik, iw, gates = jnp.split(w_in, pts, axis=-1)
    misc = jnp.pad(jnp.concatenate([ik, ga, iw], axis=-1),
                   ((0, 0), (0, LANES - IDX_DIM - GLA_LOWRANK - IDX_HEADS)))
    wa = w_in[:, :pts[3]].astype(BF16)
    wb = jnp.concatenate([dq, iq, gates, dkv, misc], axis=-1).astype(BF16)
    widths = tuple(wd if wd is not None else 2 * d for _, wd, _ in _PROJ_OUT)
    assert wa.shape[1] == sum(widths[:4]) and wa.shape[1] + wb.shape[1] == sum(widths)
    out_widths = tuple(IDX_HEADS * LANES if name == "iq" else wd for wd, (name, _, _) in zip(widths, _PROJ_OUT))
    outs = pl.pallas_call(
        functools.partial(_inproj_kernel, widths=widths),
        grid=(m // tm,),
        in_specs=[pl.BlockSpec((tm, d), lambda i: (i, 0)),
                  _const_spec((1, d)),
                  _const_spec((1, DSA_LATENT)),
                  _const_spec(wa.shape),
                  _const_spec(wb.shape)],
        out_specs=[pl.BlockSpec((tm // DSA_TQ, wd, DSA_TQ), lambda i: (i, 0, 0)) if name in _PROJ_TRANSPOSED
                   else pl.BlockSpec((tm, wd), lambda i: (i, 0)) for wd, (name, _, _) in zip(out_widths, _PROJ_OUT)],
        out_shape=[jax.ShapeDtypeStruct((m // DSA_TQ, wd, DSA_TQ) if name in _PROJ_TRANSPOSED else (m, wd), dt)
                   for wd, (name, _, dt) in zip(out_widths, _PROJ_OUT)],
        compiler_params=_params(("parallel",)),
        name="inproj",
    )(h, g.reshape(1, d), ckv_g.reshape(1, DSA_LATENT), wa, wb)
    return dict(zip([n for n, _, _ in _PROJ_OUT], outs))


def _gla_scores_safe(qs, k, b, a_ref):
    C, SUB = GLA_CHUNK, GLA_SUB
    sub_t = lax.broadcasted_iota(jnp.int32, (SUB, 1), 0)
    out_lane = lax.broadcasted_iota(jnp.int32, (SUB, C), 1)
    for i in range(C // SUB):
        r0 = i * SUB
        b_i = b[r0:r0 + SUB]
        q_i = qs[r0:r0 + SUB]
        if i == 0:
            a_row = jnp.zeros((SUB, C), F32)
        else:
            beta = b[r0 - 1:r0]
            q_t = (q_i * jnp.exp(b_i - beta)).astype(BF16)
            k_t = (k[:r0] * jnp.exp(beta - b[:r0])).astype(BF16)
            k_t = jnp.concatenate([k_t, jnp.zeros((C - r0, GLA_DK), BF16)], axis=0)
            a_row = _nt_dot(q_t, k_t)
        for s in range(SUB):
            e = jnp.exp(jnp.where(sub_t >= s, b_i - b[r0 + s:r0 + s + 1], -jnp.inf))
            col = jnp.sum(q_i * k[r0 + s:r0 + s + 1] * e, axis=-1, keepdims=True)
            a_row = jnp.where(out_lane == r0 + s, col, a_row)
        a_ref[r0:r0 + SUB, :] = a_row


def _gla_kernel(q_ref, k_ref, v_ref, gr_ref, misc_ref, aw_ref, ab_ref, ng_ref, o_ref, st_ref, a_ref):
    C = GLA_CHUNK
    DK, DV = GLA_DK, GLA_DV

    @pl.when(pl.program_id(1) == 0)
    def _():
        st_ref[...] = jnp.zeros_like(st_ref)

    x = _dot_two_piece(misc_ref[0], aw_ref[...]) + ab_ref[...]
    la = (jnp.minimum(x, 0.0) - jnp.log1p(jnp.exp(-jnp.abs(x)))) / GLA_TAU
    row = lax.broadcasted_iota(jnp.int32, (C, C), 0)
    col = lax.broadcasted_iota(jnp.int32, (C, C), 1)
    causal = col <= row
    ones_tri = jnp.where(causal, 1.0, 0.0).astype(BF16)
    la_hi, la_lo = _bf16_pieces(la)
    b_all = (jnp.dot(ones_tri, la_hi, preferred_element_type=F32)
             + jnp.dot(ones_tri, la_lo, preferred_element_type=F32))

    def head(h):
        b = b_all[:, h * DK:(h + 1) * DK]
        k = k_ref[0, :, h * DK:(h + 1) * DK]
        qs = q_ref[0, :, h * DK:(h + 1) * DK] * (DK ** -0.5)
        return b, k, qs, qs * jnp.exp(b)

    bound = jnp.zeros((1, DK), F32)
    for h in range(GLA_HEADS):
        b, k, _, q_in = head(h)
        k_max = jnp.max(jnp.abs(k), axis=0, keepdims=True) * jnp.exp(-b[C - 1:C])
        bound = bound + jnp.maximum(jnp.max(jnp.abs(q_in), axis=0, keepdims=True), 1.0) * k_max
    factored_ok = jnp.sum(bound) < GLA_FACTORED_MAX

    def finish(h, b, k, q_in, scores):
        v = v_ref[0, :, h * DV:(h + 1) * DV]
        st = st_ref[h]
        o = _nt_dot(q_in.astype(BF16), st.astype(BF16))
        o = o + jnp.dot(scores.astype(BF16), v, preferred_element_type=F32)
        o = _rms(o, ng_ref[:, h * DV:(h + 1) * DV])
        gr = gr_ref[0, :, h * DV:(h + 1) * DV].astype(F32)
        o_ref[0, :, h * DV:(h + 1) * DV] = (o * (gr * jax.nn.sigmoid(gr))).astype(o_ref.dtype)
        b_last = b[C - 1:C]
        k_dec = (k * jnp.exp(b_last - b)).astype(BF16)
        st_ref[h] = st * jnp.exp(b_last) + _tn_dot(v, k_dec)

    @pl.when(factored_ok)
    def _():
        for h in range(GLA_HEADS):
            b, k, _, q_in = head(h)
            k_out = k * jnp.exp(-b)
            scores = jnp.where(causal, _nt_dot(q_in.astype(BF16), k_out.astype(BF16)), 0.0)
            finish(h, b, k, q_in, scores)

    @pl.when(jnp.logical_not(factored_ok))
    def _():
        for h in range(GLA_HEADS):
            b, k, qs, q_in = head(h)
            _gla_scores_safe(qs, k, b, a_ref)
            finish(h, b, k, q_in, a_ref[...])


def _gla(z, alpha_w, alpha_b, norm_g, batch, seq):
    C = GLA_CHUNK
    H, DK, DV = GLA_HEADS, GLA_DK, GLA_DV

    def r3(a):
        return a.reshape(batch, seq, a.shape[-1])

    aw = jnp.zeros((LANES, H * DK), F32).at[MISC_GA:MISC_GA + GLA_LOWRANK, :].set(alpha_w)
    tok = lambda w: pl.BlockSpec((1, C, w), lambda b, c: (b, c, 0))
    return pl.pallas_call(
        _gla_kernel,
        grid=(batch, seq // C),
        in_specs=[tok(H * DK), tok(H * DK), tok(H * DV), tok(H * DV), tok(LANES),
                  _const_spec((LANES, H * DK)), _const_spec((1, H * DK)), _const_spec((1, H * DV))],
        out_specs=tok(H * DV),
        out_shape=jax.ShapeDtypeStruct((batch, seq, H * DV), BF16),
        scratch_shapes=[pltpu.VMEM((H, DV, DK), F32), pltpu.VMEM((C, C), F32)],
        compiler_params=_params(("parallel", "arbitrary")),
        name="gla",
    )(r3(z["gq"]), r3(z["gk"]), r3(z["gv"]), r3(z["gr"]), r3(z["misc"]),
      aw, alpha_b.reshape(1, H * DK), norm_g.reshape(1, H * DV))


DSA_NEAR = DSA_TQ // DSA_TK + 1


def _t5_bucket(dist):
    max_exact = REL_BUCKETS // 2
    d = np.maximum(dist, 0)
    steps = (np.log(np.maximum(d, 1).astype(np.float64) / max_exact) / math.log(REL_MAX_DIST / max_exact)
             * (REL_BUCKETS - max_exact))
    live = (d > max_exact) & (steps < REL_BUCKETS - max_exact - 0.5)
    assert np.all(np.abs(steps - np.round(steps))[live] > 1e-4)
    large = np.minimum(max_exact + steps.astype(np.int32), REL_BUCKETS - 1)
    return np.where(d < max_exact, d, large).astype(np.int32)


def _dsa_bias_kernel(rb_ref, bkt_ref, o_ref):
    h = pl.program_id(0)
    bkt = bkt_ref[...]
    far = rb_ref[REL_BUCKETS - 1, h]
    acc = jnp.zeros(bkt.shape, F32)
    for j in range(REL_BUCKETS):
        acc = jnp.where(bkt == j, rb_ref[j, h] - far, acc)
    o_ref[0, 0] = jnp.zeros(bkt.shape[1:], F32)
    o_ref[0, 1:] = acc * LOG2_E


def _dsa_bias(rel_bias):
    assert DSA_TK + 1 >= REL_MAX_DIST
    s = np.arange(DSA_TK, dtype=np.int32)[:, None]
    t = np.arange(DSA_TQ, dtype=np.int32)[None, :]
    near = [_t5_bucket(t - s + DSA_TK * (1 - j)) for j in range(DSA_NEAR)]
    bkt = jnp.asarray(np.stack(near))
    n = DSA_NEAR + 1
    return pl.pallas_call(
        _dsa_bias_kernel,
        grid=(DSA_HEADS,),
        in_specs=[pl.BlockSpec(memory_space=pltpu.SMEM),
                  pl.BlockSpec((DSA_NEAR, DSA_TK, DSA_TQ), lambda h: (0, 0, 0))],
        out_specs=pl.BlockSpec((1, n, DSA_TK, DSA_TQ), lambda h: (h, 0, 0, 0)),
        out_shape=jax.ShapeDtypeStruct((DSA_HEADS, n, DSA_TK, DSA_TQ), F32),
        name="dsa_bias",
    )(rel_bias, bkt)


def _dsa_kernel(iqt_ref, mq_ref, mk_ref, ckv_ref, dqt_ref, bias_ref, o_ref,
                sc_ref, scb_ref, lg_ref, wt_ref, acc_ref, ml_ref, *, top_k):
    TQ, TK = DSA_TQ, DSA_TK
    R = TQ // TK
    H = DSA_HEADS
    qi = pl.program_id(1)
    ngroups = qi + 1
    neg_inf = -jnp.inf
    kf = float(top_k)

    wt_ref[...] = mq_ref[0].T * ((IDX_HEADS ** -0.5) * (IDX_DIM ** -0.5))

    def head_rows(ref, h):
        return ref[0, h * LANES:(h + 1) * LANES, :]

    def key_rows(ref, kt):
        return ref[0, pl.ds(pl.multiple_of(kt * TK, TK), TK), :]

    def fold(x, op):
        return op(x.reshape(TK // SUBLANES, SUBLANES, TQ), axis=0)

    k_pos = lax.broadcasted_iota(jnp.int32, (TK, TQ), 0)
    q_pos = lax.broadcasted_iota(jnp.int32, (TK, TQ), 1) + qi * TQ

    def for_tiles(tile_fn):
        def pair(g, carry):
            tile_fn(2 * g)
            tile_fn(2 * g + 1)
            return carry

        lax.fori_loop(0, ngroups // 2, pair, 0)

        @pl.when(ngroups % 2 == 1)
        def _():
            tile_fn(ngroups - 1)

    def score_tile(kt):
        keys = key_rows(mk_ref, kt).astype(BF16)
        acc = jnp.zeros((TK, TQ), F32)
        for h in range(IDX_HEADS):
            d = jnp.dot(keys, head_rows(iqt_ref, h), preferred_element_type=F32)
            acc = acc + jnp.maximum(d, 0.0) * wt_ref[MISC_IW + h:MISC_IW + h + 1, :]
        sc = jnp.where(k_pos + kt * TK <= q_pos, acc, neg_inf)
        sc_ref[kt] = sc
        scb_ref[kt] = sc.astype(BF16)

    for_tiles(score_tile)

    rows = 2 * SUBLANES

    def count_tiles(hits, dtype):
        def pair(g, acc):
            return acc + hits(2 * g) + hits(2 * g + 1)

        acc = lax.fori_loop(0, ngroups // 2, pair, jnp.zeros((rows, TQ), dtype))
        acc = lax.cond(ngroups % 2 == 1, lambda a: a + hits(ngroups - 1), lambda a: a, acc)
        return jnp.sum(acc.astype(F32), axis=0, keepdims=True)

    def count_ge(cand):
        def hits(kt):
            hit = jnp.where(sc_ref[kt] >= cand, 1.0, 0.0)
            return jnp.sum(hit.reshape(TK // rows, rows, TQ), axis=0)

        return count_tiles(hits, F32)

    def count_ge_rounded(cand):
        cand = cand.astype(BF16)

        def hits(kt):
            hit = jnp.where(scb_ref[kt] >= cand, jnp.ones((), BF16), jnp.zeros((), BF16))
            acc = hit[:rows]
            for i in range(1, TK // rows):
                acc = acc + hit[i * rows:(i + 1) * rows]
            return acc

        return count_tiles(hits, BF16)

    def val_of(key):
        return pltpu.bitcast(key ^ ((key >> 31) & 0x7FFFFFFF), F32)

    def bisect(span, count, value, lo, hi):
        def step(_, state):
            lo, hi, n_lo = state
            mid = lo + ((hi - lo) >> 1)
            n_mid = count(value(mid))
            ok = n_mid >= kf
            return jnp.where(ok, mid, lo), jnp.where(ok, hi, mid), jnp.where(ok, n_mid, n_lo)

        steps = (span - 1).bit_length()
        lo, _, n_lo = lax.fori_loop(0, steps, step, (lo, hi, jnp.full((1, TQ), kf, F32)))
        return lo, n_lo

    key16_inf = F32_INF_BITS >> 16
    coarse, _ = bisect(2 * key16_inf + 2, count_ge_rounded, lambda k16: val_of((k16 << 16) | ((k16 >> 31) & 0xFFFF)),
                       jnp.full((1, TQ), -key16_inf - 1, jnp.int32), jnp.full((1, TQ), key16_inf + 1, jnp.int32))
    key_inf = F32_INF_BITS
    key_coarse = (coarse << 16) | ((coarse >> 31) & 0xFFFF)
    fine_lo = jnp.maximum(key_coarse - (1 << 15), -key_inf - 1)
    fine_hi = jnp.minimum(key_coarse + (1 << 16), key_inf + 1)
    key_thr, n_ge = bisect((1 << 16) + (1 << 15), count_ge, val_of, fine_lo, fine_hi)
    thr = jnp.maximum(val_of(key_thr), -F32_MAX)
    has_ties = jnp.max(n_ge) > kf

    @pl.when(jnp.logical_not(has_ties))
    def _():
        def body(g, carry):
            for u in range(R):
                kt = g * R + u
                sc_ref[kt] = jnp.where(sc_ref[kt] >= thr, 0.0, neg_inf)
            return carry

        lax.fori_loop(0, ngroups, body, 0)

    @pl.when(has_ties)
    def _():
        def count_gt(g, acc):
            for u in range(R):
                acc = acc + fold(jnp.where(sc_ref[g * R + u] > thr, 1.0, 0.0), jnp.sum)
            return acc

        n_gt = lax.fori_loop(0, ngroups, count_gt, jnp.zeros((SUBLANES, TQ), F32))
        need = kf - jnp.sum(n_gt, axis=0, keepdims=True)
        incl = jnp.where(lax.broadcasted_iota(jnp.int32, (TK, TK), 1)
                         <= lax.broadcasted_iota(jnp.int32, (TK, TK), 0), 1.0, 0.0).astype(BF16)

        def body(g, seen):
            for u in range(R):
                kt = g * R + u
                sc = sc_ref[kt]
                eq = jnp.where(sc == thr, 1.0, 0.0)
                rank = seen + jnp.dot(incl, eq.astype(BF16), preferred_element_type=F32)
                keep = jnp.where(sc > thr, 1.0, jnp.where(rank <= need, eq, 0.0))
                sc_ref[kt] = jnp.where(keep > 0.0, 0.0, neg_inf)
                seen = seen + jnp.sum(eq, axis=0, keepdims=True)
            return seen

        lax.fori_loop(0, ngroups, body, jnp.zeros((1, TQ), F32))

    m_ref, l_ref = ml_ref.at[0], ml_ref.at[1]
    m_ref[...] = jnp.full(m_ref.shape, neg_inf, F32)
    l_ref[...] = jnp.zeros(l_ref.shape, F32)
    acc_ref[...] = jnp.zeros_like(acc_ref)

    def logits_tile(kt):
        kv = key_rows(ckv_ref, kt)
        near = jnp.clip(kt - (R * qi - 1), -1, DSA_NEAR - 1) + 1
        mask = sc_ref[kt]
        for h in range(H):
            lg = jnp.dot(kv, head_rows(dqt_ref, h), preferred_element_type=F32) + (mask + bias_ref[h, near])
            lg_ref[h, kt] = lg
            m_ref[h] = jnp.maximum(m_ref[h], fold(lg, jnp.max))

    for_tiles(logits_tile)
    m_row = [jnp.max(m_ref[h], axis=0, keepdims=True) for h in range(H)]

    def pv_tile(kt):
        kv = key_rows(ckv_ref, kt)
        for h in range(H):
            p = jnp.exp2(lg_ref[h, kt] - m_row[h])
            l_ref[h] += fold(p, jnp.sum)
            acc_ref[h] += _tn_dot(kv, p.astype(BF16))

    for_tiles(pv_tile)
    for h in range(H):
        out_t = acc_ref[h] / jnp.sum(l_ref[h], axis=0, keepdims=True)
        o_ref[0, :, h * LANES:(h + 1) * LANES] = out_t.T.astype(o_ref.dtype)


def _dsa(z, rel_bias, batch, seq):
    TQ, TK = DSA_TQ, DSA_TK
    nq, nk = seq // TQ, seq // TK
    top_k = min(TOPK_MAX, seq // 4)
    H = DSA_HEADS

    def r3(a):
        return a.reshape(batch, seq, a.shape[-1])

    misc = r3(z["misc"])
    return pl.pallas_call(
        functools.partial(_dsa_kernel, top_k=top_k),
        grid=(batch, nq),
        in_specs=[pl.BlockSpec((1, IDX_HEADS * LANES, TQ), lambda b, q: (b * nq + q, 0, 0)),
                  pl.BlockSpec((1, TQ, LANES), lambda b, q: (b, q, 0)),
                  pl.BlockSpec((1, seq, LANES), lambda b, q: (b, 0, 0)),
                  pl.BlockSpec((1, seq, DSA_LATENT), lambda b, q: (b, 0, 0)),
                  pl.BlockSpec((1, H * DSA_LATENT, TQ), lambda b, q: (b * nq + q, 0, 0)),
                  _const_spec((H, DSA_NEAR + 1, TK, TQ))],
        out_specs=pl.BlockSpec((1, TQ, H * DSA_LATENT), lambda b, q: (b, q, 0)),
        out_shape=jax.ShapeDtypeStruct((batch, seq, H * DSA_LATENT), BF16),
        scratch_shapes=[pltpu.VMEM((nk, TK, TQ), F32),
                        pltpu.VMEM((nk, TK, TQ), BF16),
                        pltpu.VMEM((H, nk, TK, TQ), F32),
                        pltpu.VMEM((LANES, TQ), F32),
                        pltpu.VMEM((H, DSA_LATENT, TQ), F32),
                        pltpu.VMEM((2, H, SUBLANES, TQ), F32)],
        compiler_params=_params(("parallel", "arbitrary")),
        name="dsa",
    )(z["iq"], misc, misc, r3(z["ckv"]), z["dq"], _dsa_bias(rel_bias))


def _tail_kernel(h_ref, og_ref, od_ref, gt_ref, p_ref, wa_ref, wd_ref, wo_ref, g2_ref, win_ref, wout_ref,
                 gp_ref, wg_ref, wp_ref, gf_ref, o_ref, act_ref, *, ff):
    d = h_ref.shape[-1]
    ya = jnp.dot(og_ref[...], wa_ref[...], preferred_element_type=F32)
    yd = jnp.dot(od_ref[...], wd_ref[...], preferred_element_type=F32)
    gt = gt_ref[...].astype(F32)
    mix = jax.nn.sigmoid(gt[:, :d]) * ya + jax.nn.sigmoid(gt[:, d:]) * yd
    h = h_ref[...] + jnp.dot(mix.astype(BF16), wo_ref[...], preferred_element_type=F32)
    h = _half_swiglu(h, g2_ref, win_ref, wout_ref, act_ref, ff)
    hn = _rms(h, gp_ref[...]).astype(BF16)
    gate = jax.nn.sigmoid(jnp.dot(hn, wg_ref[...], preferred_element_type=F32))
    emb = jnp.dot(p_ref[...].astype(BF16), wp_ref[...], preferred_element_type=F32)
    o_ref[...] = _rms(h + gate * emb, gf_ref[...])


def _tail(h, og, od, gates, p, wa, wd, wo, g2, w_in, w_out, gp, wg, wp, gf, tm):
    m, d = h.shape
    ff = w_out.shape[0]
    row = lambda w: pl.BlockSpec((tm, w), lambda i: (i, 0))
    vec = lambda g: g.reshape(1, d)
    weights = [w.astype(BF16) for w in (wa, wd, wo)] + [vec(g2), w_in.astype(BF16), w_out.astype(BF16),
                                                         vec(gp), wg.astype(BF16), wp.astype(BF16), vec(gf)]
    return pl.pallas_call(
        functools.partial(_tail_kernel, ff=ff),
        grid=(m // tm,),
        in_specs=[row(d), row(og.shape[1]), row(od.shape[1]), row(2 * d), row(p.shape[1])]
                 + [_const_spec(w.shape) for w in weights],
        out_specs=row(d),
        out_shape=jax.ShapeDtypeStruct((m, d), F32),
        scratch_shapes=[pltpu.VMEM((tm, ff), BF16)],
        compiler_params=_params(("parallel",)),
        name="tail",
    )(h, og, od, gates, p, *weights)


def kernel(x, p, ffn1_norm, ffn1_w_in, ffn1_w_out, mix_norm, mix_w_in, gla_alpha_w, gla_alpha_b, gla_out_norm, gla_w_out, ckv_norm, dsa_w_out, rel_bias, mix_w_out, ffn2_norm, ffn2_w_in, ffn2_w_out, ple_norm, ple_w_gate, ple_w_proj, final_norm):
    batch, seq, d = x.shape
    assert p.shape[0] == 1, "single-layer trunk"
    m = batch * seq
    tm = min(TOKEN_TILE, m)
    h = x.reshape(m, d)
    h = _ffn(h, ffn1_norm[0], ffn1_w_in[0], ffn1_w_out[0], tm)
    z = _inproj(h, mix_norm[0], mix_w_in[0], ckv_norm[0], tm)
    og = _gla(z, gla_alpha_w[0], gla_alpha_b[0], gla_out_norm[0], batch, seq)
    od = _dsa(z, rel_bias, batch, seq)
    h = _tail(h, og.reshape(m, -1), od.reshape(m, -1), z["gates"], p[0].reshape(m, -1),
              gla_w_out[0], dsa_w_out[0], mix_w_out[0], ffn2_norm[0], ffn2_w_in[0], ffn2_w_out[0],
              ple_norm[0], ple_w_gate[0], ple_w_proj[0], final_norm, tm)
    return h.reshape(batch, seq, d)
```

```python
import functools
import math

import numpy as np
import jax
import jax.numpy as jnp
from jax import lax
from jax.experimental import pallas as pl
from jax.experimental.pallas import tpu as pltpu

F32 = jnp.float32
BF16 = jnp.bfloat16

EPS = 1e-6
GLA_HEADS = 4
GLA_DK = 128
GLA_DV = 256
GLA_LOWRANK = 16
GLA_TAU = 16.0
DSA_HEADS = 8
DSA_LATENT = 128
IDX_HEADS = 8
IDX_DIM = 64
TOPK_MAX = 256
REL_BUCKETS = 32
REL_MAX_DIST = 128

LANES = 128
SUBLANES = 8
VMEM_LIMIT = 56 * 1024 * 1024
LOG2_E = math.log2(math.e)
F32_INF_BITS = 0x7F800000
F32_MAX = float(np.finfo(np.float32).max)

TOKEN_TILE = 512
PROJ_COLS = 512
FFN_COLS = 512
GLA_CHUNK = 128
GLA_STEP_CHUNKS = 2
GLA_SUB = 16
GLA_FACTORED_MAX = 1e30
DSA_TQ = 256
DSA_TK = 256

MISC_IK = 0
MISC_GA = IDX_DIM
MISC_IW = IDX_DIM + GLA_LOWRANK


def _nt_dot(a, b):
    return lax.dot_general(a, b, (((1,), (1,)), ((), ())), preferred_element_type=F32)


def _tn_dot(a, b):
    return lax.dot_general(a, b, (((0,), (0,)), ((), ())), preferred_element_type=F32)


def _rms(x, g):
    return x * lax.rsqrt(jnp.mean(x * x, axis=-1, keepdims=True) + EPS) * g


def _bf16_pieces(x):
    hi = x.astype(BF16)
    return hi, (x - hi.astype(F32)).astype(BF16)


def _dot_two_piece(a, b):
    a_hi, a_lo = _bf16_pieces(a)
    b_hi, b_lo = _bf16_pieces(b)
    dot = functools.partial(jnp.dot, preferred_element_type=F32)
    return dot(a_hi, b_hi) + dot(a_lo, b_hi) + dot(a_hi, b_lo)


def _const_spec(shape):
    nd = len(shape)
    return pl.BlockSpec(shape, lambda *_: (0,) * nd, pipeline_mode=pl.Buffered(1))


def _params(sem):
    return pltpu.CompilerParams(dimension_semantics=sem, vmem_limit_bytes=VMEM_LIMIT)


def _half_swiglu(x, g_ref, win_ref, wout_ref, act_ref, ff):
    xn = _rms(x, g_ref[...]).astype(BF16)
    for c0 in range(0, ff, FFN_COLS):
        cw = min(FFN_COLS, ff - c0)
        gate = jnp.dot(xn, win_ref[:, c0:c0 + cw], preferred_element_type=F32)
        up = jnp.dot(xn, win_ref[:, ff + c0:ff + c0 + cw], preferred_element_type=F32)
        act_ref[:, c0:c0 + cw] = (gate * jax.nn.sigmoid(gate) * up).astype(BF16)
    return x + 0.5 * jnp.dot(act_ref[...], wout_ref[...], preferred_element_type=F32)


def _ffn_kernel(h_ref, g_ref, win_ref, wout_ref, o_ref, act_ref, *, ff):
    o_ref[...] = _half_swiglu(h_ref[...], g_ref, win_ref, wout_ref, act_ref, ff)


def _ffn(h, g, w_in, w_out, tm):
    m, d = h.shape
    ff = w_out.shape[0]
    assert ff % LANES == 0
    return pl.pallas_call(
        functools.partial(_ffn_kernel, ff=ff),
        grid=(m // tm,),
        in_specs=[pl.BlockSpec((tm, d), lambda i: (i, 0)),
                  _const_spec((1, d)),
                  _const_spec((d, 2 * ff)),
                  _const_spec((ff, d))],
        out_specs=pl.BlockSpec((tm, d), lambda i: (i, 0)),
        out_shape=jax.ShapeDtypeStruct((m, d), F32),
        scratch_shapes=[pltpu.VMEM((tm, ff), BF16)],
        compiler_params=_params(("parallel",)),
        name="ffn",
    )(h, g.reshape(1, d), w_in.astype(BF16), w_out.astype(BF16))


_PROJ_OUT = (("gq", GLA_HEADS * GLA_DK, F32), ("gk", GLA_HEADS * GLA_DK, F32),
             ("gv", GLA_HEADS * GLA_DV, BF16), ("gr", GLA_HEADS * GLA_DV, BF16),
             ("dq", DSA_HEADS * DSA_LATENT, BF16), ("iq", IDX_HEADS * IDX_DIM, BF16),
             ("gates", None, BF16), ("ckv", DSA_LATENT, BF16), ("misc", LANES, F32))
_PROJ_TRANSPOSED = ("dq", "iq")


def _inproj_kernel(h_ref, g_ref, ckvg_ref, wa_ref, wb_ref, *out_refs, widths):
    xn = _rms(h_ref[...], g_ref[...]).astype(BF16)
    w_ref, col = wa_ref, 0
    for (name, _, dt), width, o_ref in zip(_PROJ_OUT, widths, out_refs):
        if col == wa_ref.shape[1] and w_ref is wa_ref:
            w_ref, col = wb_ref, 0
        for c0 in range(0, width, PROJ_COLS):
            cw = min(PROJ_COLS, width - c0)
            z = jnp.dot(xn, w_ref[:, col + c0:col + c0 + cw], preferred_element_type=F32)
            if name == "ckv":
                z = _rms(z, ckvg_ref[...])
            elif name == "dq":
                z = z * (DSA_LATENT ** -0.5 * LOG2_E)
            if name in _PROJ_TRANSPOSED:
                zt = z.T
                for j in range(z.shape[0] // DSA_TQ):
                    block = zt[:, j * DSA_TQ:(j + 1) * DSA_TQ].astype(dt)
                    if name == "iq":
                        assert c0 == 0 and cw == width
                        for h in range(IDX_HEADS):
                            o_ref[j, h * LANES:h * LANES + IDX_DIM, :] = block[h * IDX_DIM:(h + 1) * IDX_DIM]
                            o_ref[j, h * LANES + IDX_DIM:(h + 1) * LANES, :] = jnp.zeros(
                                (LANES - IDX_DIM, DSA_TQ), dt)
                    else:
                        o_ref[j, c0:c0 + cw, :] = block
            else:
                o_ref[:, c0:c0 + cw] = z.astype(dt)
        col += width


def _inproj(h, g, w_in, ckv_g, tm):
    m, d = h.shape
    sizes = (GLA_HEADS * GLA_DK, GLA_HEADS * GLA_DK, GLA_HEADS * GLA_DV, GLA_HEADS * GLA_DV, GLA_LOWRANK,
             DSA_HEADS * DSA_LATENT, DSA_LATENT, IDX_HEADS * IDX_DIM, IDX_DIM, IDX_HEADS, 2 * d)
    pts = np.cumsum(sizes)[:-1].tolist()
    _, _, _, _, ga, dq, dkv, iq, ik, iw, gates = jnp.split(w_in, pts, axis=-1)
    misc = jnp.pad(jnp.concatenate([ik, ga, iw], axis=-1),
                   ((0, 0), (0, LANES - IDX_DIM - GLA_LOWRANK - IDX_HEADS)))
    wa = w_in[:, :pts[3]].astype(BF16)
    wb = jnp.concatenate([dq, iq, gates, dkv, misc], axis=-1).astype(BF16)
    widths = tuple(wd if wd is not None else 2 * d for _, wd, _ in _PROJ_OUT)
    assert wa.shape[1] == sum(widths[:4]) and wa.shape[1] + wb.shape[1] == sum(widths)
    out_widths = tuple(IDX_HEADS * LANES if name == "iq" else wd for wd, (name, _, _) in zip(widths, _PROJ_OUT))
    outs = pl.pallas_call(
        functools.partial(_inproj_kernel, widths=widths),
        grid=(m // tm,),
        in_specs=[pl.BlockSpec((tm, d), lambda i: (i, 0)),
                  _const_spec((1, d)),
                  _const_spec((1, DSA_LATENT)),
                  _const_spec(wa.shape),
                  _const_spec(wb.shape)],
        out_specs=[pl.BlockSpec((tm // DSA_TQ, wd, DSA_TQ), lambda i: (i, 0, 0)) if name in _PROJ_TRANSPOSED
                   else pl.BlockSpec((tm, wd), lambda i: (i, 0)) for wd, (name, _, _) in zip(out_widths, _PROJ_OUT)],
        out_shape=[jax.ShapeDtypeStruct((m // DSA_TQ, wd, DSA_TQ) if name in _PROJ_TRANSPOSED else (m, wd), dt)
                   for wd, (name, _, dt) in zip(out_widths, _PROJ_OUT)],
        compiler_params=_params(("parallel",)),
        name="inproj",
    )(h, g.reshape(1, d), ckv_g.reshape(1, DSA_LATENT), wa, wb)
    return dict(zip([n for n, _, _ in _PROJ_OUT], outs))


def _gla_scores_safe(qs, k, b, a_ref):
    C, SUB = GLA_CHUNK, GLA_SUB
    sub_t = lax.broadcasted_iota(jnp.int32, (SUB, 1), 0)
    out_lane = lax.broadcasted_iota(jnp.int32, (SUB, C), 1)
    for i in range(C // SUB):
        r0 = i * SUB
        b_i = b[r0:r0 + SUB]
        q_i = qs[r0:r0 + SUB]
        if i == 0:
            a_row = jnp.zeros((SUB, C), F32)
        else:
            beta = b[r0 - 1:r0]
            q_t = (q_i * jnp.exp(b_i - beta)).astype(BF16)
            k_t = (k[:r0] * jnp.exp(beta - b[:r0])).astype(BF16)
            k_t = jnp.concatenate([k_t, jnp.zeros((C - r0, GLA_DK), BF16)], axis=0)
            a_row = _nt_dot(q_t, k_t)
        for s in range(SUB):
            e = jnp.exp(jnp.where(sub_t >= s, b_i - b[r0 + s:r0 + s + 1], -jnp.inf))
            col = jnp.sum(q_i * k[r0 + s:r0 + s + 1] * e, axis=-1, keepdims=True)
            a_row = jnp.where(out_lane == r0 + s, col, a_row)
        a_ref[r0:r0 + SUB, :] = a_row


def _gla_kernel(q_ref, k_ref, v_ref, gr_ref, misc_ref, aw_ref, ab_ref, ng_ref, o_ref, st_ref, a_ref):
    @pl.when(pl.program_id(1) == 0)
    def _():
        st_ref[...] = jnp.zeros_like(st_ref)

    for j in range(GLA_STEP_CHUNKS):
        rows = lambda ref: ref.at[:, j * GLA_CHUNK:(j + 1) * GLA_CHUNK, :]
        _gla_chunk(rows(q_ref), rows(k_ref), rows(v_ref), rows(gr_ref), rows(misc_ref), aw_ref, ab_ref, ng_ref,
                   rows(o_ref), st_ref, a_ref)


def _gla_chunk(q_ref, k_ref, v_ref, gr_ref, misc_ref, aw_ref, ab_ref, ng_ref, o_ref, st_ref, a_ref):
    C = GLA_CHUNK
    DK, DV = GLA_DK, GLA_DV

    x = _dot_two_piece(misc_ref[0], aw_ref[...]) + ab_ref[...]
    la = (jnp.minimum(x, 0.0) - jnp.log1p(jnp.exp(-jnp.abs(x)))) / GLA_TAU
    row = lax.broadcasted_iota(jnp.int32, (C, C), 0)
    col = lax.broadcasted_iota(jnp.int32, (C, C), 1)
    causal = col <= row
    ones_tri = jnp.where(causal, 1.0, 0.0).astype(BF16)
    la_hi, la_lo = _bf16_pieces(la)
    b_all = (jnp.dot(ones_tri, la_hi, preferred_element_type=F32)
             + jnp.dot(ones_tri, la_lo, preferred_element_type=F32))

    def head(h):
        b = b_all[:, h * DK:(h + 1) * DK]
        k = k_ref[0, :, h * DK:(h + 1) * DK]
        qs = q_ref[0, :, h * DK:(h + 1) * DK] * (DK ** -0.5)
        return b, k, qs, qs * jnp.exp(b)

    bound = jnp.zeros((1, DK), F32)
    for h in range(GLA_HEADS):
        b, k, _, q_in = head(h)
        k_max = jnp.max(jnp.abs(k), axis=0, keepdims=True) * jnp.exp(-b[C - 1:C])
        bound = bound + jnp.maximum(jnp.max(jnp.abs(q_in), axis=0, keepdims=True), 1.0) * k_max
    factored_ok = jnp.sum(bound) < GLA_FACTORED_MAX

    def finish(h, b, k, q_in, scores):
        v = v_ref[0, :, h * DV:(h + 1) * DV]
        st = st_ref[h]
        o = _nt_dot(q_in.astype(BF16), st.astype(BF16))
        o = o + jnp.dot(scores.astype(BF16), v, preferred_element_type=F32)
        o = _rms(o, ng_ref[:, h * DV:(h + 1) * DV])
        gr = gr_ref[0, :, h * DV:(h + 1) * DV].astype(F32)
        o_ref[0, :, h * DV:(h + 1) * DV] = (o * (gr * jax.nn.sigmoid(gr))).astype(o_ref.dtype)
        b_last = b[C - 1:C]
        k_dec = (k * jnp.exp(b_last - b)).astype(BF16)
        st_ref[h] = st * jnp.exp(b_last) + _tn_dot(v, k_dec)

    @pl.when(factored_ok)
    def _():
        for h in range(GLA_HEADS):
            b, k, _, q_in = head(h)
            k_out = k * jnp.exp(-b)
            scores = jnp.where(causal, _nt_dot(q_in.astype(BF16), k_out.astype(BF16)), 0.0)
            finish(h, b, k, q_in, scores)

    @pl.when(jnp.logical_not(factored_ok))
    def _():
        for h in range(GLA_HEADS):
            b, k, qs, q_in = head(h)
            _gla_scores_safe(qs, k, b, a_ref)
            finish(h, b, k, q_in, a_ref[...])


def _gla(z, alpha_w, alpha_b, norm_g, batch, seq):
    C = GLA_CHUNK
    H, DK, DV = GLA_HEADS, GLA_DK, GLA_DV

    def r3(a):
        return a.reshape(batch, seq, a.shape[-1])

    aw = jnp.zeros((LANES, H * DK), F32).at[MISC_GA:MISC_GA + GLA_LOWRANK, :].set(alpha_w)
    step = C * GLA_STEP_CHUNKS
    tok = lambda w: pl.BlockSpec((1, step, w), lambda b, c: (b, c, 0))
    return pl.pallas_call(
        _gla_kernel,
        grid=(batch, seq // step),
        in_specs=[tok(H * DK), tok(H * DK), tok(H * DV), tok(H * DV), tok(LANES),
                  _const_spec((LANES, H * DK)), _const_spec((1, H * DK)), _const_spec((1, H * DV))],
        out_specs=tok(H * DV),
        out_shape=jax.ShapeDtypeStruct((batch, seq, H * DV), BF16),
        scratch_shapes=[pltpu.VMEM((H, DV, DK), F32), pltpu.VMEM((C, C), F32)],
        compiler_params=_params(("parallel", "arbitrary")),
        name="gla",
    )(r3(z["gq"]), r3(z["gk"]), r3(z["gv"]), r3(z["gr"]), r3(z["misc"]),
      aw, alpha_b.reshape(1, H * DK), norm_g.reshape(1, H * DV))


DSA_NEAR = DSA_TQ // DSA_TK + 1


def _t5_bucket(dist):
    max_exact = REL_BUCKETS // 2
    d = np.maximum(dist, 0)
    steps = (np.log(np.maximum(d, 1).astype(np.float64) / max_exact) / math.log(REL_MAX_DIST / max_exact)
             * (REL_BUCKETS - max_exact))
    live = (d > max_exact) & (steps < REL_BUCKETS - max_exact - 0.5)
    assert np.all(np.abs(steps - np.round(steps))[live] > 1e-4)
    large = np.minimum(max_exact + steps.astype(np.int32), REL_BUCKETS - 1)
    return np.where(d < max_exact, d, large).astype(np.int32)


def _dsa_bias_kernel(rb_ref, bkt_ref, o_ref):
    h = pl.program_id(0)
    bkt = bkt_ref[...]
    far = rb_ref[REL_BUCKETS - 1, h]
    acc = jnp.zeros(bkt.shape, F32)
    for j in range(REL_BUCKETS):
        acc = jnp.where(bkt == j, rb_ref[j, h] - far, acc)
    o_ref[0, 0] = jnp.zeros(bkt.shape[1:], F32)
    o_ref[0, 1:] = acc * LOG2_E


def _dsa_bias(rel_bias):
    assert DSA_TK + 1 >= REL_MAX_DIST
    s = np.arange(DSA_TK, dtype=np.int32)[:, None]
    t = np.arange(DSA_TQ, dtype=np.int32)[None, :]
    near = [_t5_bucket(t - s + DSA_TK * (1 - j)) for j in range(DSA_NEAR)]
    bkt = jnp.asarray(np.stack(near))
    n = DSA_NEAR + 1
    return pl.pallas_call(
        _dsa_bias_kernel,
        grid=(DSA_HEADS,),
        in_specs=[pl.BlockSpec(memory_space=pltpu.SMEM),
                  pl.BlockSpec((DSA_NEAR, DSA_TK, DSA_TQ), lambda h: (0, 0, 0))],
        out_specs=pl.BlockSpec((1, n, DSA_TK, DSA_TQ), lambda h: (h, 0, 0, 0)),
        out_shape=jax.ShapeDtypeStruct((DSA_HEADS, n, DSA_TK, DSA_TQ), F32),
        name="dsa_bias",
    )(rel_bias, bkt)


def _dsa_kernel(iqt_ref, mq_ref, mk_ref, ckv_ref, dqt_ref, bias_ref, o_ref,
                sc_ref, scb_ref, lg_ref, wt_ref, acc_ref, ml_ref, *, top_k):
    TQ, TK = DSA_TQ, DSA_TK
    R = TQ // TK
    H = DSA_HEADS
    qi = pl.program_id(1)
    ngroups = qi + 1
    neg_inf = -jnp.inf
    kf = float(top_k)

    wt_ref[...] = mq_ref[0].T * ((IDX_HEADS ** -0.5) * (IDX_DIM ** -0.5))

    def head_rows(ref, h):
        return ref[0, h * LANES:(h + 1) * LANES, :]

    def key_rows(ref, kt):
        return ref[0, pl.ds(pl.multiple_of(kt * TK, TK), TK), :]

    def fold(x, op):
        return op(x.reshape(TK // SUBLANES, SUBLANES, TQ), axis=0)

    k_pos = lax.broadcasted_iota(jnp.int32, (TK, TQ), 0)
    q_pos = lax.broadcasted_iota(jnp.int32, (TK, TQ), 1) + qi * TQ

    def for_tiles(tile_fn):
        def pair(g, carry):
            tile_fn(2 * g)
            tile_fn(2 * g + 1)
            return carry

        lax.fori_loop(0, ngroups // 2, pair, 0)

        @pl.when(ngroups % 2 == 1)
        def _():
            tile_fn(ngroups - 1)

    def score_tile(kt):
        keys = key_rows(mk_ref, kt).astype(BF16)
        acc = jnp.zeros((TK, TQ), F32)
        for h in range(IDX_HEADS):
            d = jnp.dot(keys, head_rows(iqt_ref, h), preferred_element_type=F32)
            acc = acc + jnp.maximum(d, 0.0) * wt_ref[MISC_IW + h:MISC_IW + h + 1, :]
        sc = jnp.where(k_pos + kt * TK <= q_pos, acc, neg_inf)
        sc_ref[kt] = sc
        scb_ref[kt] = sc.astype(BF16)

    for_tiles(score_tile)

    rows = 2 * SUBLANES

    def count_tiles(hits, dtype):
        def pair(g, acc):
            return acc + hits(2 * g) + hits(2 * g + 1)

        acc = lax.fori_loop(0, ngroups // 2, pair, jnp.zeros((rows, TQ), dtype))
        acc = lax.cond(ngroups % 2 == 1, lambda a: a + hits(ngroups - 1), lambda a: a, acc)
        return jnp.sum(acc.astype(F32), axis=0, keepdims=True)

    def count_ge(cand):
        def hits(kt):
            hit = jnp.where(sc_ref[kt] >= cand, 1.0, 0.0)
            return jnp.sum(hit.reshape(TK // rows, rows, TQ), axis=0)

        return count_tiles(hits, F32)

    def count_ge_rounded(cand):
        cand = cand.astype(BF16)

        def hits(kt):
            hit = jnp.where(scb_ref[kt] >= cand, jnp.ones((), BF16), jnp.zeros((), BF16))
            acc = hit[:rows]
            for i in range(1, TK // rows):
                acc = acc + hit[i * rows:(i + 1) * rows]
            return acc

        return count_tiles(hits, BF16)

    def val_of(key):
        return pltpu.bitcast(key ^ ((key >> 31) & 0x7FFFFFFF), F32)

    def bisect(span, count, value, lo, hi):
        def step(_, state):
            lo, hi, n_lo = state
            mid = lo + ((hi - lo) >> 1)
            n_mid = count(value(mid))
            ok = n_mid >= kf
            return jnp.where(ok, mid, lo), jnp.where(ok, hi, mid), jnp.where(ok, n_mid, n_lo)

        steps = (span - 1).bit_length()
        lo, _, n_lo = lax.fori_loop(0, steps, step, (lo, hi, jnp.full((1, TQ), kf, F32)))
        return lo, n_lo

    key16_inf = F32_INF_BITS >> 16
    coarse, _ = bisect(2 * key16_inf + 2, count_ge_rounded, lambda k16: val_of((k16 << 16) | ((k16 >> 31) & 0xFFFF)),
                       jnp.full((1, TQ), -key16_inf - 1, jnp.int32), jnp.full((1, TQ), key16_inf + 1, jnp.int32))
    key_inf = F32_INF_BITS
    key_coarse = (coarse << 16) | ((coarse >> 31) & 0xFFFF)
    fine_lo = jnp.maximum(key_coarse - (1 << 15), -key_inf - 1)
    fine_hi = jnp.minimum(key_coarse + (1 << 16), key_inf + 1)
    key_thr, n_ge = bisect((1 << 16) + (1 << 15), count_ge, val_of, fine_lo, fine_hi)
    thr = jnp.maximum(val_of(key_thr), -F32_MAX)
    has_ties = jnp.max(n_ge) > kf

    @pl.when(jnp.logical_not(has_ties))
    def _():
        def body(g, carry):
            for u in range(R):
                kt = g * R + u
                sc_ref[kt] = jnp.where(sc_ref[kt] >= thr, 0.0, neg_inf)
            return carry

        lax.fori_loop(0, ngroups, body, 0)

    @pl.when(has_ties)
    def _():
        def count_gt(g, acc):
            for u in range(R):
                acc = acc + fold(jnp.where(sc_ref[g * R + u] > thr, 1.0, 0.0), jnp.sum)
            return acc

        n_gt = lax.fori_loop(0, ngroups, count_gt, jnp.zeros((SUBLANES, TQ), F32))
        need = kf - jnp.sum(n_gt, axis=0, keepdims=True)
        incl = jnp.where(lax.broadcasted_iota(jnp.int32, (TK, TK), 1)
                         <= lax.broadcasted_iota(jnp.int32, (TK, TK), 0), 1.0, 0.0).astype(BF16)

        def body(g, seen):
            for u in range(R):
                kt = g * R + u
                sc = sc_ref[kt]
                eq = jnp.where(sc == thr, 1.0, 0.0)
                rank = seen + jnp.dot(incl, eq.astype(BF16), preferred_element_type=F32)
                keep = jnp.where(sc > thr, 1.0, jnp.where(rank <= need, eq, 0.0))
                sc_ref[kt] = jnp.where(keep > 0.0, 0.0, neg_inf)
                seen = seen + jnp.sum(eq, axis=0, keepdims=True)
            return seen

        lax.fori_loop(0, ngroups, body, jnp.zeros((1, TQ), F32))

    m_ref, l_ref = ml_ref.at[0], ml_ref.at[1]
    m_ref[...] = jnp.full(m_ref.shape, neg_inf, F32)
    l_ref[...] = jnp.zeros(l_ref.shape, F32)
    acc_ref[...] = jnp.zeros_like(acc_ref)

    def logits_tile(kt):
        kv = key_rows(ckv_ref, kt)
        near = jnp.clip(kt - (R * qi - 1), -1, DSA_NEAR - 1) + 1
        mask = sc_ref[kt]
        for h in range(H):
            lg = jnp.dot(kv, head_rows(dqt_ref, h), preferred_element_type=F32) + (mask + bias_ref[h, near])
            lg_ref[h, kt] = lg
            m_ref[h] = jnp.maximum(m_ref[h], fold(lg, jnp.max))

    for_tiles(logits_tile)
    m_row = [jnp.max(m_ref[h], axis=0, keepdims=True) for h in range(H)]

    def pv_tile(kt):
        kv = key_rows(ckv_ref, kt)
        for h in range(H):
            p = jnp.exp2(lg_ref[h, kt] - m_row[h])
            l_ref[h] += fold(p, jnp.sum)
            acc_ref[h] += _tn_dot(kv, p.astype(BF16))

    for_tiles(pv_tile)
    for h in range(H):
        out_t = acc_ref[h] / jnp.sum(l_ref[h], axis=0, keepdims=True)
        o_ref[0, :, h * LANES:(h + 1) * LANES] = out_t.T.astype(o_ref.dtype)


def _dsa(z, rel_bias, batch, seq):
    TQ, TK = DSA_TQ, DSA_TK
    nq, nk = seq // TQ, seq // TK
    top_k = min(TOPK_MAX, seq // 4)
    H = DSA_HEADS

    def r3(a):
        return a.reshape(batch, seq, a.shape[-1])

    misc = r3(z["misc"])
    return pl.pallas_call(
        functools.partial(_dsa_kernel, top_k=top_k),
        grid=(batch, nq),
        in_specs=[pl.BlockSpec((1, IDX_HEADS * LANES, TQ), lambda b, q: (b * nq + q, 0, 0)),
                  pl.BlockSpec((1, TQ, LANES), lambda b, q: (b, q, 0)),
                  pl.BlockSpec((1, seq, LANES), lambda b, q: (b, 0, 0)),
                  pl.BlockSpec((1, seq, DSA_LATENT), lambda b, q: (b, 0, 0)),
                  pl.BlockSpec((1, H * DSA_LATENT, TQ), lambda b, q: (b * nq + q, 0, 0)),
                  _const_spec((H, DSA_NEAR + 1, TK, TQ))],
        out_specs=pl.BlockSpec((1, TQ, H * DSA_LATENT), lambda b, q: (b, q, 0)),
        out_shape=jax.ShapeDtypeStruct((batch, seq, H * DSA_LATENT), BF16),
        scratch_shapes=[pltpu.VMEM((nk, TK, TQ), F32),
                        pltpu.VMEM((nk, TK, TQ), BF16),
                        pltpu.VMEM((H, nk, TK, TQ), F32),
                        pltpu.VMEM((LANES, TQ), F32),
                        pltpu.VMEM((H, DSA_LATENT, TQ), F32),
                        pltpu.VMEM((2, H, SUBLANES, TQ), F32)],
        compiler_params=_params(("parallel", "arbitrary")),
        name="dsa",
    )(z["iq"], misc, misc, r3(z["ckv"]), z["dq"], _dsa_bias(rel_bias))


def _tail_kernel(h_ref, og_ref, od_ref, gt_ref, p_ref, wa_ref, wd_ref, wo_ref, g2_ref, win_ref, wout_ref,
                 gp_ref, wg_ref, wp_ref, gf_ref, o_ref, act_ref, *, ff):
    d = h_ref.shape[-1]
    ya = jnp.dot(og_ref[...], wa_ref[...], preferred_element_type=F32)
    yd = jnp.dot(od_ref[...], wd_ref[...], preferred_element_type=F32)
    gt = gt_ref[...].astype(F32)
    mix = jax.nn.sigmoid(gt[:, :d]) * ya + jax.nn.sigmoid(gt[:, d:]) * yd
    h = h_ref[...] + jnp.dot(mix.astype(BF16), wo_ref[...], preferred_element_type=F32)
    h = _half_swiglu(h, g2_ref, win_ref, wout_ref, act_ref, ff)
    hn = _rms(h, gp_ref[...]).astype(BF16)
    gate = jax.nn.sigmoid(jnp.dot(hn, wg_ref[...], preferred_element_type=F32))
    emb = jnp.dot(p_ref[...].astype(BF16), wp_ref[...], preferred_element_type=F32)
    o_ref[...] = _rms(h + gate * emb, gf_ref[...])


def _tail(h, og, od, gates, p, wa, wd, wo, g2, w_in, w_out, gp, wg, wp, gf, tm):
    m, d = h.shape
    ff = w_out.shape[0]
    row = lambda w: pl.BlockSpec((tm, w), lambda i: (i, 0))
    vec = lambda g: g.reshape(1, d)
    weights = [w.astype(BF16) for w in (wa, wd, wo)] + [vec(g2), w_in.astype(BF16), w_out.astype(BF16),
                                                         vec(gp), wg.astype(BF16), wp.astype(BF16), vec(gf)]
    return pl.pallas_call(
        functools.partial(_tail_kernel, ff=ff),
        grid=(m // tm,),
        in_specs=[row(d), row(og.shape[1]), row(od.shape[1]), row(2 * d), row(p.shape[1])]
                 + [_const_spec(w.shape) for w in weights],
        out_specs=row(d),
        out_shape=jax.ShapeDtypeStruct((m, d), F32),
        scratch_shapes=[pltpu.VMEM((tm, ff), BF16)],
        compiler_params=_params(("parallel",)),
        name="tail",
    )(h, og, od, gates, p, *weights)


def kernel(x, p, ffn1_norm, ffn1_w_in, ffn1_w_out, mix_norm, mix_w_in, gla_alpha_w, gla_alpha_b, gla_out_norm, gla_w_out, ckv_norm, dsa_w_out, rel_bias, mix_w_out, ffn2_norm, ffn2_w_in, ffn2_w_out, ple_norm, ple_w_gate, ple_w_proj, final_norm):
    batch, seq, d = x.shape
    assert p.shape[0] == 1, "single-layer trunk"
    m = batch * seq
    tm = min(TOKEN_TILE, m)
    h = x.reshape(m, d)
    h = _ffn(h, ffn1_norm[0], ffn1_w_in[0], ffn1_w_out[0], tm)
    z = _inproj(h, mix_norm[0], mix_w_in[0], ckv_norm[0], tm)
    og = _gla(z, gla_alpha_w[0], gla_alpha_b[0], gla_out_norm[0], batch, seq)
    od = _dsa(z, rel_bias, batch, seq)
    h = _tail(h, og.reshape(m, -1), od.reshape(m, -1), z["gates"], p[0].reshape(m, -1),
              gla_w_out[0], dsa_w_out[0], mix_w_out[0], ffn2_norm[0], ffn2_w_in[0], ffn2_w_out[0],
              ple_norm[0], ple_w_gate[0], ple_w_proj[0], final_norm, tm)
    return h.reshape(batch, seq, d)
```
